```python
import jax, jax.numpy as jnp
from jax import lax
import numpy as np

D_MODEL = 1024
BATCH = 2
SEQ = 16384
DEPTH = 4
DEC_BATCH = 16
DEC_SEQ = 16
PAST_LEN = 1024

CHUNK = 64
HEAD_DIM = 64
H_A = 8
H_B = 8
A_WIDTH = H_A * HEAD_DIM
B_WIDTH = H_B * HEAD_DIM
C_CONV = 512
CONV_W = 31
A_PAST_CHUNKS = 8
A_WINDOW = A_PAST_CHUNKS * CHUNK
REL_MIN = -(CHUNK - 1)
REL_MAX = 128
REL_SIZE = REL_MAX - REL_MIN + 1
SB_BLOCK = 128
D_FF = 2816
N_BRANCH = 3
IN_SPLITS = (A_WIDTH, A_WIDTH, A_WIDTH, B_WIDTH, B_WIDTH, B_WIDTH, C_CONV, C_CONV, N_BRANCH * D_MODEL)
IN_COLS = sum(IN_SPLITS)
RMS_EPS = 1e-6
LN_EPS = 1e-5
NEG_INF = -1e30

kernel_name = "hybrid_streaming_encoder_step"


def rmsnorm(x, g):
    x32 = x.astype(jnp.float32)
    y = x32 * lax.rsqrt(jnp.mean(x32 * x32, axis=-1, keepdims=True) + RMS_EPS)
    return (y * g.astype(jnp.float32)).astype(x.dtype)


def ffn_half(x, g, w_gate, w_up, w_down):
    h = rmsnorm(x, g)
    return x + 0.5 * ((jax.nn.silu(h @ w_gate) * (h @ w_up)) @ w_down)


def in_projection(h, w_in, b_gate):
    z = h @ w_in
    qa, ka, va, qb, kb, vb, u_lin, u_gate, graw = jnp.split(z, np.cumsum(IN_SPLITS)[:-1], axis=-1)
    n, t = h.shape[0], h.shape[1]
    ha = lambda y: y.reshape(n, t, H_A, HEAD_DIM)
    hb = lambda y: y.reshape(n, t, H_B, HEAD_DIM)
    u = u_lin * jax.nn.sigmoid(u_gate)
    g = jax.nn.sigmoid(graw + b_gate).reshape(n, t, N_BRANCH, D_MODEL)
    return ha(qa), ha(ka), ha(va), hb(qb), hb(kb), hb(vb), u, g


def band_attention(q, k, v, rel, valid, rel_bias):
    s = jnp.einsum('ncqhd,nckhd->nchqk', q, k).astype(jnp.float32) * (HEAD_DIM ** -0.5)
    idx = np.clip(rel, REL_MIN, REL_MAX) - REL_MIN
    s = s + rel_bias.astype(jnp.float32)[:, idx]
    s = jnp.where(valid[None, :, None, None, :], s, NEG_INF)
    p = jax.nn.softmax(s, axis=-1)
    return jnp.einsum('nchqk,nckhd->ncqhd', p.astype(v.dtype), v)


def stick_breaking(q, k, v, q_pos, k_pos):
    z = jnp.einsum('nqhd,nkhd->nhqk', q, k).astype(jnp.float32) * (HEAD_DIM ** -0.5)
    mask = k_pos[None, :] < q_pos[:, None]
    log_keep = jnp.where(mask, jax.nn.log_sigmoid(-z), 0.0)
    r = lax.cumsum(log_keep, axis=3, reverse=True)
    a = jnp.where(mask, jnp.exp(z + r), 0.0)
    return jnp.einsum('nhqk,nkhd->nqhd', a.astype(v.dtype), v)


def stick_breaking_blocks(q, k, v):
    n, t, h, d = q.shape
    nb = t // SB_BLOCK
    tri = jnp.asarray(np.triu(np.ones((SB_BLOCK, SB_BLOCK), np.float32)).T)
    diag = np.arange(SB_BLOCK)[None, :] < np.arange(SB_BLOCK)[:, None]
    outs = []
    for i in range(nb):
        nk = (i + 1) * SB_BLOCK
        qi = q[:, i * SB_BLOCK:nk]
        kk, vv = k[:, :nk], v[:, :nk]
        mask = np.concatenate([np.ones((SB_BLOCK, i * SB_BLOCK), bool), diag], axis=1)
        z = jnp.einsum('nqhd,nkhd->nhqk', qi, kk).astype(jnp.float32) * (HEAD_DIM ** -0.5)
        lk = jnp.where(mask, jax.nn.log_sigmoid(-z), 0.0).reshape(n, h, SB_BLOCK, i + 1, SB_BLOCK)
        local = jnp.einsum('nhqcj,js->nhqcs', lk, tri)
        tot = jnp.sum(lk, axis=-1)
        later = lax.cumsum(tot, axis=3, reverse=True) - tot
        r = (local + later[..., None]).reshape(n, h, SB_BLOCK, nk)
        a = jnp.where(mask, jnp.exp(z + r), 0.0)
        outs.append(jnp.einsum('nhqk,nkhd->nqhd', a.astype(vv.dtype), vv))
    return jnp.concatenate(outs, axis=1).reshape(n, t, h * d)


def conv_module(up, conv_w, conv_b, ln_g, ln_b, w_c_out):
    w = conv_w.astype(up.dtype)[:, None, :]
    y = lax.conv_general_dilated(up, w, window_strides=(1,), padding='VALID',
                                 dimension_numbers=('NWC', 'WIO', 'NWC'),
                                 feature_group_count=C_CONV) + conv_b
    y32 = y.astype(jnp.float32)
    mu = jnp.mean(y32, axis=-1, keepdims=True)
    var = jnp.mean(jnp.square(y32 - mu), axis=-1, keepdims=True)
    y32 = (y32 - mu) * lax.rsqrt(var + LN_EPS) * ln_g.astype(jnp.float32) + ln_b.astype(jnp.float32)
    return jax.nn.silu(y32).astype(up.dtype) @ w_c_out


def merge_branches(g, oa, ob, oc, w_a_out, w_b_out, w_o):
    m = g[:, :, 0] * (oa @ w_a_out) + g[:, :, 1] * (ob @ w_b_out) + g[:, :, 2] * oc
    return m @ w_o


def mixer_prompt(h, w_in, b_gate, rel_bias, w_a_out, w_b_out, conv_w, conv_b, ln_g, ln_b, w_c_out, w_o):
    n, t, _ = h.shape
    qa, ka, va, qb, kb, vb, u, g = in_projection(h, w_in, b_gate)
    nc = t // CHUNK

    def band(y):
        yc = y.reshape(n, nc, CHUNK, H_A, HEAD_DIM)
        yp = jnp.pad(yc, ((0, 0), (A_PAST_CHUNKS, 0), (0, 0), (0, 0), (0, 0)))
        return jnp.concatenate([yp[:, o:o + nc] for o in range(A_PAST_CHUNKS + 1)], axis=2)

    j = np.arange((A_PAST_CHUNKS + 1) * CHUNK)
    rel = np.arange(CHUNK)[:, None] - (j[None, :] - A_WINDOW)
    valid = (np.arange(nc)[:, None] - A_PAST_CHUNKS + j[None, :] // CHUNK) >= 0
    oa = band_attention(qa.reshape(n, nc, CHUNK, H_A, HEAD_DIM), band(ka), band(va),
                        rel, valid, rel_bias).reshape(n, t, A_WIDTH)
    ob = stick_breaking_blocks(qb, kb, vb)
    up = jnp.pad(u, ((0, 0), (CONV_W - 1, 0), (0, 0)))
    oc = conv_module(up, conv_w, conv_b, ln_g, ln_b, w_c_out)
    y = merge_branches(g, oa, ob, oc, w_a_out, w_b_out, w_o)
    keep = min(A_WINDOW, t)
    return y, (ka[:, t - keep:], va[:, t - keep:], kb, vb, up[:, t:])


def mixer_sample(h, ca_k, ca_v, cb_k, cb_v, c_conv, w_in, b_gate, rel_bias, w_a_out, w_b_out,
                 conv_w, conv_b, ln_g, ln_b, w_c_out, w_o):
    n, t, _ = h.shape
    a_len = ca_k.shape[1]
    past = cb_k.shape[1]
    qa, ka, va, qb, kb, vb, u, g = in_projection(h, w_in, b_gate)
    q_pos = past + np.arange(t)
    ka_pos = np.concatenate([past - a_len + np.arange(a_len), q_pos])
    rel = q_pos[:, None] - ka_pos[None, :]
    valid = np.ones((1, a_len + t), dtype=bool)
    ka_all = jnp.concatenate([ca_k, ka], axis=1)[:, None]
    va_all = jnp.concatenate([ca_v, va], axis=1)[:, None]
    oa = band_attention(qa[:, None], ka_all, va_all, rel, valid, rel_bias)[:, 0].reshape(n, t, A_WIDTH)
    kb_all = jnp.concatenate([cb_k, kb], axis=1)
    vb_all = jnp.concatenate([cb_v, vb], axis=1)
    ob = stick_breaking(qb, kb_all, vb_all, jnp.asarray(q_pos, jnp.int32),
                        jnp.arange(past + t, dtype=jnp.int32)).reshape(n, t, B_WIDTH)
    up = jnp.concatenate([c_conv, u], axis=1)
    oc = conv_module(up, conv_w, conv_b, ln_g, ln_b, w_c_out)
    y = merge_branches(g, oa, ob, oc, w_a_out, w_b_out, w_o)
    return y, (ka, va, kb, vb, up[:, t:])


def setup_inputs(seed: int = 0) -> dict:
    key = jax.random.key(seed)
    ks = jax.random.split(key, 32)
    nrm = lambda k, shape, scale: jax.random.normal(k, shape, jnp.float32) * scale
    a_len = min(A_WINDOW, PAST_LEN)
    return {
        "x_prompt": nrm(ks[0], (BATCH, SEQ, D_MODEL), 1.0),
        "x_sample": nrm(ks[1], (DEC_BATCH, DEC_SEQ, D_MODEL), 1.0),
        "cache_a_k": nrm(ks[2], (DEPTH, DEC_BATCH, a_len, H_A, HEAD_DIM), 1.0),
        "cache_a_v": nrm(ks[3], (DEPTH, DEC_BATCH, a_len, H_A, HEAD_DIM), 1.0),
        "cache_b_k": nrm(ks[4], (DEPTH, DEC_BATCH, PAST_LEN, H_B, HEAD_DIM), 1.0),
        "cache_b_v": nrm(ks[5], (DEPTH, DEC_BATCH, PAST_LEN, H_B, HEAD_DIM), 1.0),
        "state_conv": nrm(ks[6], (DEPTH, DEC_BATCH, CONV_W - 1, C_CONV), 0.5),
        "w_in": nrm(ks[7], (DEPTH, D_MODEL, IN_COLS), D_MODEL ** -0.5),
        "b_gate": nrm(ks[8], (DEPTH, N_BRANCH * D_MODEL), 0.01),
        "rel_bias": nrm(ks[9], (DEPTH, H_A, REL_SIZE), 0.1),
        "w_a_out": nrm(ks[10], (DEPTH, A_WIDTH, D_MODEL), A_WIDTH ** -0.5),
        "w_b_out": nrm(ks[11], (DEPTH, B_WIDTH, D_MODEL), B_WIDTH ** -0.5),
        "conv_w": nrm(ks[12], (DEPTH, CONV_W, C_CONV), CONV_W ** -0.5),
        "conv_b": nrm(ks[13], (DEPTH, C_CONV), 0.01),
        "conv_ln_g": 1.0 + nrm(ks[14], (DEPTH, C_CONV), 0.01),
        "conv_ln_b": nrm(ks[15], (DEPTH, C_CONV), 0.01),
        "w_c_out": nrm(ks[16], (DEPTH, C_CONV, D_MODEL), C_CONV ** -0.5),
        "w_o": nrm(ks[17], (DEPTH, D_MODEL, D_MODEL), D_MODEL ** -0.5),
        "ln_ffn1": 1.0 + nrm(ks[18], (DEPTH, D_MODEL), 0.01),
        "ffn1_w_gate": nrm(ks[19], (DEPTH, D_MODEL, D_FF), D_MODEL ** -0.5),
        "ffn1_w_up": nrm(ks[20], (DEPTH, D_MODEL, D_FF), D_MODEL ** -0.5),
        "ffn1_w_down": nrm(ks[21], (DEPTH, D_FF, D_MODEL), D_FF ** -0.5),
        "ln_mix": 1.0 + nrm(ks[22], (DEPTH, D_MODEL), 0.01),
        "ln_ffn2": 1.0 + nrm(ks[23], (DEPTH, D_MODEL), 0.01),
        "ffn2_w_gate": nrm(ks[24], (DEPTH, D_MODEL, D_FF), D_MODEL ** -0.5),
        "ffn2_w_up": nrm(ks[25], (DEPTH, D_MODEL, D_FF), D_MODEL ** -0.5),
        "ffn2_w_down": nrm(ks[26], (DEPTH, D_FF, D_MODEL), D_FF ** -0.5),
        "final_norm": 1.0 + nrm(ks[27], (D_MODEL,), 0.01),
    }


def reference(x_prompt, x_sample, cache_a_k, cache_a_v, cache_b_k, cache_b_v, state_conv,
              w_in, b_gate, rel_bias, w_a_out, w_b_out, conv_w, conv_b, conv_ln_g, conv_ln_b,
              w_c_out, w_o, ln_ffn1, ffn1_w_gate, ffn1_w_up, ffn1_w_down, ln_mix, ln_ffn2,
              ffn2_w_gate, ffn2_w_up, ffn2_w_down, final_norm):
    xp, xs = x_prompt, x_sample
    p_st, s_st = [], []
    for l in range(DEPTH):
        mix_w = (w_in[l], b_gate[l], rel_bias[l], w_a_out[l], w_b_out[l], conv_w[l], conv_b[l],
                 conv_ln_g[l], conv_ln_b[l], w_c_out[l], w_o[l])
        xp = ffn_half(xp, ln_ffn1[l], ffn1_w_gate[l], ffn1_w_up[l], ffn1_w_down[l])
        yp, stp = mixer_prompt(rmsnorm(xp, ln_mix[l]), *mix_w)
        xp = ffn_half(xp + yp, ln_ffn2[l], ffn2_w_gate[l], ffn2_w_up[l], ffn2_w_down[l])
        xs = ffn_half(xs, ln_ffn1[l], ffn1_w_gate[l], ffn1_w_up[l], ffn1_w_down[l])
        ys, sts = mixer_sample(rmsnorm(xs, ln_mix[l]), cache_a_k[l], cache_a_v[l], cache_b_k[l],
                               cache_b_v[l], state_conv[l], *mix_w)
        xs = ffn_half(xs + ys, ln_ffn2[l], ffn2_w_gate[l], ffn2_w_up[l], ffn2_w_down[l])
        p_st.append(stp)
        s_st.append(sts)
    y_prompt = rmsnorm(xp, final_norm)
    y_sample = rmsnorm(xs, final_norm)
    stk = lambda sts, i: jnp.stack([s[i] for s in sts], axis=0)
    return (y_prompt, y_sample,
            stk(p_st, 0), stk(p_st, 1), stk(p_st, 2), stk(p_st, 3), stk(p_st, 4),
            stk(s_st, 0), stk(s_st, 1), stk(s_st, 2), stk(s_st, 3), stk(s_st, 4))
```

```python
import functools

import numpy as np
import jax
import jax.numpy as jnp
from jax import lax
from jax.experimental import pallas as pl
from jax.experimental.pallas import tpu as pltpu

F32 = jnp.float32
BF16 = jnp.bfloat16

HEAD_DIM = 64
CHUNK = 64
A_PAST_CHUNKS = 8
REL_MIN = -(CHUNK - 1)
REL_MAX = 128
CONV_W = 31
RMS_EPS = 1e-6
LN_EPS = 1e-5
NEG_INF = -1e30

LANES = 128
BAND_BLOCK = 2 * CHUNK
BAND_NBLK = A_PAST_CHUNKS * CHUNK // BAND_BLOCK + 1
SB_BLOCK = 256
SB_HEADS = 4
CONV_HIST = 32
VMEM_LIMIT = 56 * 1024 * 1024


def _params(n_axes, vmem=VMEM_LIMIT):
    return pltpu.CompilerParams(dimension_semantics=("arbitrary",) * n_axes, vmem_limit_bytes=vmem)


def _resident(shape):
    nd = len(shape)
    return pl.BlockSpec(shape, lambda *_: (0,) * nd, pipeline_mode=pl.Buffered(1))


def _row_tile(m, want):
    tm = want
    while tm > 16 and m % tm:
        tm //= 2
    assert m % tm == 0
    return tm


def _rms(x, g):
    return x * lax.rsqrt(jnp.mean(x * x, axis=-1, keepdims=True) + RMS_EPS) * g


def _sigmoid(x):
    return 1.0 / (1.0 + jnp.exp(-x))


def _dot(a, b):
    return jnp.dot(a, b, preferred_element_type=F32)


def _dot_nt(a, b):
    return lax.dot_general(a, b, (((1,), (1,)), ((), ())), preferred_element_type=F32)


def _ffn_body(x_ref, g_ref, wg_ref, wu_ref, wd_ref, fg_ref, o_ref, *, tf, final):
    x = x_ref[...]
    h = _rms(x, g_ref[...]).astype(BF16)
    acc = None
    for c in range(wg_ref.shape[1] // tf):
        a = _dot(h, wg_ref[:, c * tf:(c + 1) * tf])
        b = _dot(h, wu_ref[:, c * tf:(c + 1) * tf])
        act = (a * _sigmoid(a) * b).astype(BF16)
        d = _dot(act, wd_ref[c * tf:(c + 1) * tf, :])
        acc = d if acc is None else acc + d
    y = x + 0.5 * acc
    if final:
        y = _rms(y, fg_ref[...])
    o_ref[...] = y


def _ffn(x, g, wg, wu, wd, fg, *, final):
    m, d = x.shape
    d_ff = wg.shape[1]
    tm = _row_tile(m, 512)
    tf = d_ff // 2 if (d_ff // 2) % LANES == 0 else d_ff
    row = pl.BlockSpec((tm, d), lambda i: (i, 0))
    return pl.pallas_call(
        functools.partial(_ffn_body, tf=tf, final=final),
        out_shape=jax.ShapeDtypeStruct((m, d), F32),
        grid=(m // tm,),
        in_specs=[row, _resident((1, d)), _resident((d, d_ff)), _resident((d, d_ff)),
                  _resident((d_ff, d)), _resident((1, d))],
        out_specs=row,
        compiler_params=_params(1),
        name="ffn_half",
    )(x, g.reshape(1, d), wg, wu, wd, fg.reshape(1, d))


def _inproj_body(x_ref, g_ref, w_ref, bg_ref, qa_ref, ka32_ref, va32_ref, ka16_ref, va16_ref,
                 qb_ref, kb32_ref, vb32_ref, kb16_ref, vb16_ref, u_ref, gate_ref, *, width, d_model):
    h = _rms(x_ref[...], g_ref[...]).astype(BF16)

    def mm(col, n):
        return _dot(h, w_ref[:, col:col + n])

    scale = HEAD_DIM ** -0.5
    col = 0
    for q_ref, k32_ref, v32_ref, k16_ref, v16_ref in (
            (qa_ref, ka32_ref, va32_ref, ka16_ref, va16_ref),
            (qb_ref, kb32_ref, vb32_ref, kb16_ref, vb16_ref)):
        q_ref[...] = (mm(col, width) * scale).astype(BF16)
        k = mm(col + width, width)
        k32_ref[...] = k
        k16_ref[...] = k.astype(BF16)
        v = mm(col + 2 * width, width)
        v32_ref[...] = v
        v16_ref[...] = v.astype(BF16)
        col += 3 * width
    lin = mm(col, width)
    gt = mm(col + width, width)
    u_ref[...] = lin * _sigmoid(gt)
    col += 2 * width
    for j in range(3):
        graw = mm(col + j * d_model, d_model) + bg_ref[:, j * d_model:(j + 1) * d_model]
        gate_ref[:, j * d_model:(j + 1) * d_model] = _sigmoid(graw).astype(BF16)


def _inproj(x, g, w_in, b_gate, *, width):
    m, d = x.shape
    n_cols = w_in.shape[1]
    tm = _row_tile(m, 256)
    row = lambda n: pl.BlockSpec((tm, n), lambda i: (i, 0))
    sds = lambda n, dt: jax.ShapeDtypeStruct((m, n), dt)
    mixer = [sds(width, BF16), sds(width, F32), sds(width, F32), sds(width, BF16), sds(width, BF16)]
    mixer_specs = [row(width)] * 5
    return pl.pallas_call(
        functools.partial(_inproj_body, width=width, d_model=d),
        out_shape=mixer + mixer + [sds(width, F32), sds(3 * d, BF16)],
        grid=(m // tm,),
        in_specs=[row(d), _resident((1, d)), _resident((d, n_cols)), _resident((1, 3 * d))],
        out_specs=mixer_specs + mixer_specs + [row(width), row(3 * d)],
        compiler_params=_params(1),
        name="in_projection",
    )(x, g.reshape(1, d), w_in, b_gate.reshape(1, 3 * d))


def _band_body(q_ref, *rest, qoff, n_heads):
    k_refs = rest[:BAND_NBLK]
    v_refs = rest[BAND_NBLK:2 * BAND_NBLK]
    bias_ref, o_ref = rest[2 * BAND_NBLK:]
    blk = pl.program_id(1) + qoff
    low = lax.broadcasted_iota(jnp.int32, (BAND_BLOCK, LANES), 1) < HEAD_DIM
    for hp in range(n_heads // 2):
        lanes = slice(hp * LANES, (hp + 1) * LANES)
        q = q_ref[0, :, lanes].astype(F32)
        ks = [r[0, :, lanes] for r in k_refs]
        vs = [r[0, :, lanes] for r in v_refs]
        outs = []
        for hh in range(2):
            head = 2 * hp + hh
            qm = jnp.where(low if hh == 0 else jnp.logical_not(low), q, 0.0).astype(BF16)
            s = []
            for j in range(BAND_NBLK):
                sj = _dot_nt(qm, ks[j]) + bias_ref[head, j]
                if j < BAND_NBLK - 1:
                    sj = jnp.where(blk + j - (BAND_NBLK - 1) >= 0, sj, NEG_INF)
                s.append(sj)
            mx = functools.reduce(jnp.maximum, [jnp.max(sj, axis=-1, keepdims=True) for sj in s])
            p = [jnp.exp(sj - mx) for sj in s]
            den = functools.reduce(jnp.add, [jnp.sum(pj, axis=-1, keepdims=True) for pj in p])
            o = functools.reduce(jnp.add, [_dot(pj.astype(BF16), vj) for pj, vj in zip(p, vs)])
            outs.append(o * (1.0 / den))
        o_ref[0, :, lanes] = jnp.where(low, outs[0], outs[1]).astype(BF16)


def _band_attention(q, k, v, bias, *, qoff):
    n, tq, w = q.shape
    n_heads = w // HEAD_DIM
    qspec = pl.BlockSpec((1, BAND_BLOCK, w), lambda b, i: (b, i, 0))

    def kspec(j):
        return pl.BlockSpec((1, BAND_BLOCK, w),
                            lambda b, i: (b, jnp.maximum(i + qoff + j - (BAND_NBLK - 1), 0), 0))

    kv_specs = [kspec(j) for j in range(BAND_NBLK)]
    return pl.pallas_call(
        functools.partial(_band_body, qoff=qoff, n_heads=n_heads),
        out_shape=jax.ShapeDtypeStruct((n, tq, w), BF16),
        grid=(n, tq // BAND_BLOCK),
        in_specs=[qspec] + kv_specs + kv_specs + [_resident(bias.shape)],
        out_specs=qspec,
        compiler_params=_params(2),
        name="band_attention",
    )(q, *([k] * BAND_NBLK), *([v] * BAND_NBLK), bias)


def _band_bias(rel_bias, kv_len=None):
    r = np.arange(BAND_BLOCK)[:, None]
    c = np.arange(BAND_BLOCK)[None, :]
    idx = np.zeros((BAND_NBLK, BAND_BLOCK, BAND_BLOCK), np.int32)
    ok = np.zeros((BAND_NBLK, BAND_BLOCK, BAND_BLOCK), bool)
    for j in range(BAND_NBLK):
        rel = r - c + BAND_BLOCK * (BAND_NBLK - 1 - j)
        idx[j] = np.clip(rel, REL_MIN, REL_MAX) - REL_MIN
        dchunk = (A_PAST_CHUNKS + r // CHUNK) - (c // CHUNK + (BAND_BLOCK // CHUNK) * j)
        ok[j] = (dchunk >= 0) & (dchunk <= A_PAST_CHUNKS)
        if kv_len is not None:
            ok[j] &= (BAND_BLOCK * j + c) < kv_len
    table = rel_bias.astype(F32)[:, :, idx]
    return jnp.where(ok[None, None], table, NEG_INF)


def _sb_body(q_ref, k_ref, v_ref, tri_ref, o_ref, *, qoff):
    blk = pl.program_id(2) + qoff
    width = SB_HEADS * HEAD_DIM
    lane = lax.broadcasted_iota(jnp.int32, (SB_BLOCK, width), 1)
    head_lanes = [(lane >= h * HEAD_DIM) & (lane < (h + 1) * HEAD_DIM) for h in range(SB_HEADS)]
    q = q_ref[0].astype(F32)
    qh = [jnp.where(m, q, 0.0).astype(BF16) for m in head_lanes]
    tri = tri_ref[...]
    row = lax.broadcasted_iota(jnp.int32, (SB_BLOCK, SB_BLOCK), 0)
    col = lax.broadcasted_iota(jnp.int32, (SB_BLOCK, SB_BLOCK), 1)
    causal = col < row

    def sweep(kb, acc, later, diagonal):
        start = pl.multiple_of(kb * SB_BLOCK, SB_BLOCK)
        k = k_ref[0, pl.ds(start, SB_BLOCK), :]
        v = v_ref[0, pl.ds(start, SB_BLOCK), :]
        new_later = []
        for h in range(SB_HEADS):
            z = _dot_nt(qh[h], k)
            log_keep = -(jnp.maximum(z, 0.0) + jnp.log(1.0 + jnp.exp(-jnp.abs(z))))
            if diagonal:
                log_keep = jnp.where(causal, log_keep, 0.0)
            suffix = _dot(log_keep.astype(BF16), tri)
            a = jnp.exp(z + suffix + later[h])
            if diagonal:
                a = jnp.where(causal, a, 0.0)
            pv = _dot(a.astype(BF16), v)
            acc = acc + jnp.where(head_lanes[h], pv, 0.0)
            new_later.append(later[h] + suffix[:, 0:1])
        return acc, tuple(new_later)

    acc = jnp.zeros((SB_BLOCK, width), F32)
    later = tuple(jnp.zeros((SB_BLOCK, 1), F32) for _ in range(SB_HEADS))
    acc, later = sweep(blk, acc, later, True)

    def body(step, carry):
        return sweep(blk - 1 - step, carry[0], carry[1], False)

    acc, later = lax.fori_loop(0, blk, body, (acc, later))
    o_ref[0] = acc.astype(BF16)


def _stick_breaking(q, k, v, *, qoff):
    n, tq, w = q.shape
    tk = k.shape[1]
    gw = SB_HEADS * HEAD_DIM
    tri = jnp.asarray(np.tril(np.ones((SB_BLOCK, SB_BLOCK), np.float32)), BF16)
    qspec = pl.BlockSpec((1, SB_BLOCK, gw), lambda b, g, i: (b, i, g))
    kspec = pl.BlockSpec((1, tk, gw), lambda b, g, i: (b, 0, g))
    return pl.pallas_call(
        functools.partial(_sb_body, qoff=qoff),
        out_shape=jax.ShapeDtypeStruct((n, tq, w), BF16),
        grid=(n, w // gw, tq // SB_BLOCK),
        in_specs=[qspec, kspec, kspec, _resident((SB_BLOCK, SB_BLOCK))],
        out_specs=qspec,
        compiler_params=_params(3),
        name="stick_breaking",
    )(q, k, v, tri)


def _merge_body(u_ref, hist_ref, oa_ref, ob_ref, gate_ref, x_ref, cw_ref, cb_ref, lg_ref, lb_ref,
                wc_ref, wa_ref, wb_ref, wo_ref, o_ref, up_ref, *, zero_first_hist):
    tm = u_ref.shape[1]
    d = x_ref.shape[2]
    hist = hist_ref[0]
    if zero_first_hist:
        hist = jnp.where(pl.program_id(1) > 0, hist, 0.0)
    up_ref[0:CONV_HIST, :] = hist
    up_ref[CONV_HIST:CONV_HIST + tm, :] = u_ref[0]
    first = CONV_HIST - (CONV_W - 1)
    y = None
    for w in range(CONV_W):
        term = up_ref[first + w:first + w + tm, :] * cw_ref[w:w + 1, :]
        y = term if y is None else y + term
    y = y + cb_ref[...]
    mu = jnp.mean(y, axis=-1, keepdims=True)
    yc = y - mu
    var = jnp.mean(yc * yc, axis=-1, keepdims=True)
    yn = yc * lax.rsqrt(var + LN_EPS) * lg_ref[...] + lb_ref[...]
    oc = _dot((yn * _sigmoid(yn)).astype(BF16), wc_ref[...])
    pa = _dot(oa_ref[0], wa_ref[...])
    pb = _dot(ob_ref[0], wb_ref[...])
    merged = (gate_ref[0, :, 0:d].astype(F32) * pa
              + gate_ref[0, :, d:2 * d].astype(F32) * pb
              + gate_ref[0, :, 2 * d:3 * d].astype(F32) * oc)
    o_ref[0] = x_ref[0] + _dot(merged.astype(BF16), wo_ref[...])


def _merge(u, hist, oa, ob, gate, x, cw, cb, lg, lb, wc, wa, wb, wo, *, tm, hist_from_u):
    n, t, c = u.shape
    d = x.shape[2]
    tile = lambda ch: pl.BlockSpec((1, tm, ch), lambda b, i: (b, i, 0))
    if hist_from_u:
        per = tm // CONV_HIST
        hspec = pl.BlockSpec((1, CONV_HIST, c), lambda b, i: (b, jnp.maximum(i * per - 1, 0), 0))
    else:
        hspec = pl.BlockSpec((1, CONV_HIST, c), lambda b, i: (b, 0, 0))
    vec = lambda a: a.reshape(1, -1)
    return pl.pallas_call(
        functools.partial(_merge_body, zero_first_hist=hist_from_u),
        out_shape=jax.ShapeDtypeStruct((n, t, d), F32),
        grid=(n, t // tm),
        in_specs=[tile(c), hspec, tile(oa.shape[2]), tile(ob.shape[2]), tile(3 * d), tile(d),
                  _resident(cw.shape), _resident((1, c)), _resident((1, c)), _resident((1, c)),
                  _resident(wc.shape), _resident(wa.shape), _resident(wb.shape), _resident(wo.shape)],
        out_specs=tile(d),
        scratch_shapes=[pltpu.VMEM((CONV_HIST + tm, c), F32)],
        compiler_params=_params(2),
        name="conv_merge",
    )(u, hist, oa, ob, gate, x, cw, vec(cb), vec(lg), vec(lb), wc, wa, wb, wo)


def _pad_rows(a, rows):
    return jnp.pad(a, ((0, 0), (0, rows - a.shape[1]), (0, 0)))


def kernel(x_prompt, x_sample, cache_a_k, cache_a_v, cache_b_k, cache_b_v, state_conv, w_in, b_gate,
           rel_bias, w_a_out, w_b_out, conv_w, conv_b, conv_ln_g, conv_ln_b, w_c_out, w_o, ln_ffn1,
           ffn1_w_gate, ffn1_w_up, ffn1_w_down, ln_mix, ln_ffn2, ffn2_w_gate, ffn2_w_up, ffn2_w_down,
           final_norm):
    depth = w_in.shape[0]
    nb, t, d = x_prompt.shape
    ns, ts, _ = x_sample.shape
    a_len = cache_a_k.shape[2]
    past = cache_b_k.shape[2]
    h_a, h_b = cache_a_k.shape[3], cache_b_k.shape[3]
    wa_width, wb_width = h_a * HEAD_DIM, h_b * HEAD_DIM
    c_conv = state_conv.shape[3]
    assert wa_width == wb_width == c_conv
    assert t % 512 == 0 and a_len % BAND_BLOCK == 0 and past % SB_BLOCK == 0
    assert a_len == A_PAST_CHUNKS * CHUNK and past % CHUNK == 0 and ts <= CHUNK and ts % 16 == 0
    assert t >= a_len

    bf = lambda a: a.astype(BF16)
    w_in16, wa16, wb16, wc16, wo16 = bf(w_in), bf(w_a_out), bf(w_b_out), bf(w_c_out), bf(w_o)
    f1g, f1u, f1d = bf(ffn1_w_gate), bf(ffn1_w_up), bf(ffn1_w_down)
    f2g, f2u, f2d = bf(ffn2_w_gate), bf(ffn2_w_up), bf(ffn2_w_down)

    bias_p = _band_bias(rel_bias)
    bias_s = _band_bias(rel_bias, kv_len=a_len + ts)
    band_qoff_s = a_len // BAND_BLOCK
    sb_qoff_s = past // SB_BLOCK

    xp = x_prompt.reshape(nb * t, d)
    xs = x_sample.reshape(ns * ts, d)
    p_state, s_state = [], []
    for l in range(depth):
        last = l == depth - 1
        mix_w = (conv_w[l], conv_b[l], conv_ln_g[l], conv_ln_b[l], wc16[l], wa16[l], wb16[l], wo16[l])

        xp = _ffn(xp, ln_ffn1[l], f1g[l], f1u[l], f1d[l], final_norm, final=False)
        (qa, ka32, va32, ka16, va16, qb, kb32, vb32, kb16, vb16, u, gate) = _inproj(
            xp, ln_mix[l], w_in16[l], b_gate[l], width=wa_width)
        seq = lambda a: a.reshape(nb, t, a.shape[-1])
        oa = _band_attention(seq(qa), seq(ka16), seq(va16), bias_p[l], qoff=0)
        ob = _stick_breaking(seq(qb), seq(kb16), seq(vb16), qoff=0)
        u3 = seq(u)
        xp = _merge(u3, u3, oa, ob, seq(gate), seq(xp), *mix_w, tm=512, hist_from_u=True)
        xp = _ffn(xp.reshape(nb * t, d), ln_ffn2[l], f2g[l], f2u[l], f2d[l], final_norm, final=last)
        heads = lambda a, h: a.reshape(nb, t, h, HEAD_DIM)
        p_state.append((heads(ka32, h_a)[:, t - a_len:], heads(va32, h_a)[:, t - a_len:],
                        heads(kb32, h_b), heads(vb32, h_b), u3[:, t - (CONV_W - 1):]))

        xs = _ffn(xs, ln_ffn1[l], f1g[l], f1u[l], f1d[l], final_norm, final=False)
        (qa, ka32, va32, ka16, va16, qb, kb32, vb32, kb16, vb16, u, gate) = _inproj(
            xs, ln_mix[l], w_in16[l], b_gate[l], width=wa_width)
        seq = lambda a: a.reshape(ns, ts, a.shape[-1])
        a_rows = a_len + BAND_BLOCK
        cat_a = lambda cache, new: _pad_rows(
            jnp.concatenate([bf(cache.reshape(ns, a_len, wa_width)), seq(new)], axis=1), a_rows)
        oa = _band_attention(_pad_rows(seq(qa), BAND_BLOCK), cat_a(cache_a_k[l], ka16),
                             cat_a(cache_a_v[l], va16), bias_s[l], qoff=band_qoff_s)[:, :ts]
        b_rows = past + SB_BLOCK
        cat_b = lambda cache, new: _pad_rows(
            jnp.concatenate([bf(cache.reshape(ns, past, wb_width)), seq(new)], axis=1), b_rows)
        ob = _stick_breaking(_pad_rows(seq(qb), SB_BLOCK), cat_b(cache_b_k[l], kb16),
                             cat_b(cache_b_v[l], vb16), qoff=sb_qoff_s)[:, :ts]
        u3 = seq(u)
        hist = jnp.pad(state_conv[l], ((0, 0), (CONV_HIST - (CONV_W - 1), 0), (0, 0)))
        xs = _merge(u3, hist, oa, ob, seq(gate), seq(xs), *mix_w, tm=ts, hist_from_u=False)
        xs = _ffn(xs.reshape(ns * ts, d), ln_ffn2[l], f2g[l], f2u[l], f2d[l], final_norm, final=last)
        heads = lambda a, h: a.reshape(ns, ts, h, HEAD_DIM)
        up = jnp.concatenate([state_conv[l], u3], axis=1)
        s_state.append((heads(ka32, h_a), heads(va32, h_a), heads(kb32, h_b), heads(vb32, h_b),
                        up[:, ts:]))

    stack = lambda states, i: jnp.stack([s[i] for s in states], axis=0)
    return (xp.reshape(nb, t, d), xs.reshape(ns, ts, d),
            stack(p_state, 0), stack(p_state, 1), stack(p_state, 2), stack(p_state, 3), stack(p_state, 4),
            stack(s_state, 0), stack(s_state, 1), stack(s_state, 2), stack(s_state, 3), stack(s_state, 4))
```

```python
import functools

import numpy as np
import jax
import jax.numpy as jnp
from jax import lax
from jax.experimental import pallas as pl
from jax.experimental.pallas import tpu as pltpu

F32 = jnp.float32
BF16 = jnp.bfloat16

HEAD_DIM = 64
CHUNK = 64
A_PAST_CHUNKS = 8
REL_MIN = -(CHUNK - 1)
REL_MAX = 128
CONV_W = 31
RMS_EPS = 1e-6
LN_EPS = 1e-5
NEG_INF = -1e30

LANES = 128
BAND_BLOCK = 2 * CHUNK
BAND_NBLK = A_PAST_CHUNKS * CHUNK // BAND_BLOCK + 1
SB_BLOCK = 256
SB_HEADS = 4
SB_DEAD = -120.0
CONV_HIST = 32
VMEM_LIMIT = 56 * 1024 * 1024


def _params(n_axes, vmem=VMEM_LIMIT):
    return pltpu.CompilerParams(dimension_semantics=("arbitrary",) * n_axes, vmem_limit_bytes=vmem)


def _resident(shape):
    nd = len(shape)
    return pl.BlockSpec(shape, lambda *_: (0,) * nd, pipeline_mode=pl.Buffered(1))


def _row_tile(m, want):
    tm = want
    while tm > 16 and m % tm:
        tm //= 2
    assert m % tm == 0
    return tm


def _rms(x, g):
    return x * lax.rsqrt(jnp.mean(x * x, axis=-1, keepdims=True) + RMS_EPS) * g


def _sigmoid(x):
    return 1.0 / (1.0 + jnp.exp(-x))


def _dot(a, b):
    return jnp.dot(a, b, preferred_element_type=F32)


def _dot_nt(a, b):
    return lax.dot_general(a, b, (((1,), (1,)), ((), ())), preferred_element_type=F32)


def _ffn_body(x_ref, g_ref, wg_ref, wu_ref, wd_ref, fg_ref, o_ref, *, tf, final):
    x = x_ref[...]
    h = _rms(x, g_ref[...]).astype(BF16)
    acc = None
    for c in range(wg_ref.shape[1] // tf):
        a = _dot(h, wg_ref[:, c * tf:(c + 1) * tf])
        b = _dot(h, wu_ref[:, c * tf:(c + 1) * tf])
        act = (a * _sigmoid(a) * b).astype(BF16)
        d = _dot(act, wd_ref[c * tf:(c + 1) * tf, :])
        acc = d if acc is None else acc + d
    y = x + 0.5 * acc
    if final:
        y = _rms(y, fg_ref[...])
    o_ref[...] = y


def _ffn(x, g, wg, wu, wd, fg, *, final):
    m, d = x.shape
    d_ff = wg.shape[1]
    tm = _row_tile(m, 512)
    tf = d_ff // 2 if (d_ff // 2) % LANES == 0 else d_ff
    row = pl.BlockSpec((tm, d), lambda i: (i, 0))
    return pl.pallas_call(
        functools.partial(_ffn_body, tf=tf, final=final),
        out_shape=jax.ShapeDtypeStruct((m, d), F32),
        grid=(m // tm,),
        in_specs=[row, _resident((1, d)), _resident((d, d_ff)), _resident((d, d_ff)),
                  _resident((d_ff, d)), _resident((1, d))],
        out_specs=row,
        compiler_params=_params(1),
        name="ffn_half",
    )(x, g.reshape(1, d), wg, wu, wd, fg.reshape(1, d))


def _inproj_body(x_ref, g_ref, w_ref, bg_ref, qa_ref, ka32_ref, va32_ref, ka16_ref, va16_ref,
                 qb_ref, kb32_ref, vb32_ref, kb16_ref, vb16_ref, u_ref, gate_ref, *, width, d_model):
    h = _rms(x_ref[...], g_ref[...]).astype(BF16)

    def mm(col, n):
        return _dot(h, w_ref[:, col:col + n])

    scale = HEAD_DIM ** -0.5
    col = 0
    for q_ref, k32_ref, v32_ref, k16_ref, v16_ref in (
            (qa_ref, ka32_ref, va32_ref, ka16_ref, va16_ref),
            (qb_ref, kb32_ref, vb32_ref, kb16_ref, vb16_ref)):
        q_ref[...] = (mm(col, width) * scale).astype(BF16)
        k = mm(col + width, width)
        k32_ref[...] = k
        k16_ref[...] = k.astype(BF16)
        v = mm(col + 2 * width, width)
        v32_ref[...] = v
        v16_ref[...] = v.astype(BF16)
        col += 3 * width
    lin = mm(col, width)
    gt = mm(col + width, width)
    u_ref[...] = lin * _sigmoid(gt)
    col += 2 * width
    for j in range(3):
        graw = mm(col + j * d_model, d_model) + bg_ref[:, j * d_model:(j + 1) * d_model]
        gate_ref[:, j * d_model:(j + 1) * d_model] = _sigmoid(graw).astype(BF16)


def _inproj(x, g, w_in, b_gate, *, width):
    m, d = x.shape
    n_cols = w_in.shape[1]
    tm = _row_tile(m, 256)
    row = lambda n: pl.BlockSpec((tm, n), lambda i: (i, 0))
    sds = lambda n, dt: jax.ShapeDtypeStruct((m, n), dt)
    mixer = [sds(width, BF16), sds(width, F32), sds(width, F32), sds(width, BF16), sds(width, BF16)]
    mixer_specs = [row(width)] * 5
    return pl.pallas_call(
        functools.partial(_inproj_body, width=width, d_model=d),
        out_shape=mixer + mixer + [sds(width, F32), sds(3 * d, BF16)],
        grid=(m // tm,),
        in_specs=[row(d), _resident((1, d)), _resident((d, n_cols)), _resident((1, 3 * d))],
        out_specs=mixer_specs + mixer_specs + [row(width), row(3 * d)],
        compiler_params=_params(1),
        name="in_projection",
    )(x, g.reshape(1, d), w_in, b_gate.reshape(1, 3 * d))


def _band_body(q_ref, k_ref, v_ref, bias_ref, o_ref, *, qoff, n_heads):
    blk = pl.program_id(1) + qoff
    low = lax.broadcasted_iota(jnp.int32, (BAND_BLOCK, LANES), 1) < HEAD_DIM
    first = [blk + j - (BAND_NBLK - 1) for j in range(BAND_NBLK)]
    starts = [pl.multiple_of(jnp.maximum(f, 0) * BAND_BLOCK, BAND_BLOCK) for f in first]
    for hp in range(n_heads // 2):
        lanes = slice(hp * LANES, (hp + 1) * LANES)
        q = q_ref[0, :, lanes].astype(F32)
        q2 = jnp.concatenate([jnp.where(low, q, 0.0), jnp.where(low, 0.0, q)], axis=0).astype(BF16)
        s = []
        for j in range(BAND_NBLK):
            sj = _dot_nt(q2, k_ref[0, pl.ds(starts[j], BAND_BLOCK), lanes]) + bias_ref[hp, j]
            if j < BAND_NBLK - 1:
                sj = jnp.where(first[j] >= 0, sj, NEG_INF)
            s.append(sj)
        mx = functools.reduce(jnp.maximum, [jnp.max(sj, axis=-1, keepdims=True) for sj in s])
        p = [jnp.exp(sj - mx) for sj in s]
        den = functools.reduce(jnp.add, [jnp.sum(pj, axis=-1, keepdims=True) for pj in p])
        o = functools.reduce(jnp.add, [
            _dot(pj.astype(BF16), v_ref[0, pl.ds(starts[j], BAND_BLOCK), lanes])
            for j, pj in enumerate(p)])
        o = o * (1.0 / den)
        o_ref[0, :, lanes] = jnp.where(low, o[:BAND_BLOCK], o[BAND_BLOCK:]).astype(BF16)


def _band_attention(q, k, v, bias, *, qoff):
    n, tq, w = q.shape
    tk = k.shape[1]
    n_heads = w // HEAD_DIM
    qspec = pl.BlockSpec((1, BAND_BLOCK, w), lambda b, i: (b, i, 0))
    kspec = pl.BlockSpec((1, tk, w), lambda b, i: (b, 0, 0), pipeline_mode=pl.Buffered(1))
    return pl.pallas_call(
        functools.partial(_band_body, qoff=qoff, n_heads=n_heads),
        out_shape=jax.ShapeDtypeStruct((n, tq, w), BF16),
        grid=(n, tq // BAND_BLOCK),
        in_specs=[qspec, kspec, kspec, _resident(bias.shape)],
        out_specs=qspec,
        compiler_params=_params(2),
        name="band_attention",
    )(q, k, v, bias)


def _band_bias(rel_bias, kv_len=None):
    nb = BAND_BLOCK
    r = np.arange(nb)[:, None]
    c = np.arange(nb)[None, :]
    ok = np.zeros((BAND_NBLK, nb, nb), bool)
    m = np.arange(2 * nb)
    diff = np.where(m < nb, -m, 2 * nb - m)
    line_idx = np.zeros((BAND_NBLK, 2 * nb), np.int32)
    for j in range(BAND_NBLK):
        line_idx[j] = np.clip(diff + nb * (BAND_NBLK - 1 - j), REL_MIN, REL_MAX) - REL_MIN
        dchunk = (A_PAST_CHUNKS + r // CHUNK) - (c // CHUNK + (nb // CHUNK) * j)
        ok[j] = (dchunk >= 0) & (dchunk <= A_PAST_CHUNKS)
        if kv_len is not None:
            ok[j] &= (nb * j + c) < kv_len
    line = rel_bias.astype(F32)[:, :, line_idx]
    lead = line.shape[:3]
    table = jnp.tile(line, (1, 1, 1, nb))[..., :nb * (2 * nb - 1)]
    table = table.reshape(*lead, nb, 2 * nb - 1)[..., :nb]
    table = jnp.where(ok[None, None], table, NEG_INF)
    depth, heads = lead[:2]
    table = table.reshape(depth, heads // 2, 2, BAND_NBLK, nb, nb).transpose(0, 1, 3, 2, 4, 5)
    return table.reshape(depth, heads // 2, BAND_NBLK, 2 * nb, nb)


def _sb_body(q_ref, k_ref, v_ref, tri_ref, o_ref, *, qoff):
    blk = pl.program_id(2) + qoff
    width = SB_HEADS * HEAD_DIM
    lane = lax.broadcasted_iota(jnp.int32, (SB_BLOCK, width), 1)
    head_lanes = [(lane >= h * HEAD_DIM) & (lane < (h + 1) * HEAD_DIM) for h in range(SB_HEADS)]
    q = q_ref[0].astype(F32)
    qh = [jnp.where(m, q, 0.0).astype(BF16) for m in head_lanes]
    tri = tri_ref[...]
    row = lax.broadcasted_iota(jnp.int32, (SB_BLOCK, SB_BLOCK), 0)
    col = lax.broadcasted_iota(jnp.int32, (SB_BLOCK, SB_BLOCK), 1)
    causal = col < row

    def sweep(kb, acc, later, diagonal):
        start = pl.multiple_of(kb * SB_BLOCK, SB_BLOCK)
        k = k_ref[0, pl.ds(start, SB_BLOCK), :]
        v = v_ref[0, pl.ds(start, SB_BLOCK), :]
        new_later = []
        for h in range(SB_HEADS):
            z = _dot_nt(qh[h], k)
            log_keep = -(jnp.maximum(z, 0.0) + jnp.log(1.0 + jnp.exp(-jnp.abs(z))))
            if diagonal:
                log_keep = jnp.where(causal, log_keep, 0.0)
            suffix = _dot(log_keep.astype(BF16), tri)
            a = jnp.exp(z + suffix + later[h])
            if diagonal:
                a = jnp.where(causal, a, 0.0)
            pv = _dot(a.astype(BF16), v)
            acc = acc + jnp.where(head_lanes[h], pv, 0.0)
            new_later.append(later[h] + suffix[:, 0:1])
        return acc, tuple(new_later)

    acc = jnp.zeros((SB_BLOCK, width), F32)
    later = tuple(jnp.zeros((SB_BLOCK, 1), F32) for _ in range(SB_HEADS))
    acc, later = sweep(blk, acc, later, True)

    def body(carry):
        _, kb, acc, later = carry
        acc, later = sweep(kb, acc, later, False)
        alive = jnp.max(functools.reduce(jnp.maximum, later)) > SB_DEAD
        more = jnp.logical_and(kb > 0, alive).astype(jnp.int32)
        return more, kb - 1, acc, later

    start = (blk > 0).astype(jnp.int32)
    _, _, acc, later = lax.while_loop(lambda c: c[0] > 0, body, (start, blk - 1, acc, later))
    o_ref[0] = acc.astype(BF16)


def _stick_breaking(q, k, v, *, qoff):
    n, tq, w = q.shape
    tk = k.shape[1]
    gw = SB_HEADS * HEAD_DIM
    tri = jnp.asarray(np.tril(np.ones((SB_BLOCK, SB_BLOCK), np.float32)), BF16)
    qspec = pl.BlockSpec((1, SB_BLOCK, gw), lambda b, g, i: (b, i, g))
    kspec = pl.BlockSpec((1, tk, gw), lambda b, g, i: (b, 0, g))
    return pl.pallas_call(
        functools.partial(_sb_body, qoff=qoff),
        out_shape=jax.ShapeDtypeStruct((n, tq, w), BF16),
        grid=(n, w // gw, tq // SB_BLOCK),
        in_specs=[qspec, kspec, kspec, _resident((SB_BLOCK, SB_BLOCK))],
        out_specs=qspec,
        compiler_params=_params(3),
        name="stick_breaking",
    )(q, k, v, tri)


def _merge_body(u_ref, hist_ref, oa_ref, ob_ref, gate_ref, x_ref, cw_ref, cb_ref, lg_ref, lb_ref,
                wc_ref, wa_ref, wb_ref, wo_ref, o_ref, up_ref, *, zero_first_hist):
    tm = u_ref.shape[1]
    d = x_ref.shape[2]
    hist = hist_ref[0]
    if zero_first_hist:
        hist = jnp.where(pl.program_id(1) > 0, hist, 0.0)
    up_ref[0:CONV_HIST, :] = hist
    up_ref[CONV_HIST:CONV_HIST + tm, :] = u_ref[0]
    first = CONV_HIST - (CONV_W - 1)
    y = None
    for w in range(CONV_W):
        term = up_ref[first + w:first + w + tm, :] * cw_ref[w:w + 1, :]
        y = term if y is None else y + term
    y = y + cb_ref[...]
    mu = jnp.mean(y, axis=-1, keepdims=True)
    yc = y - mu
    var = jnp.mean(yc * yc, axis=-1, keepdims=True)
    yn = yc * lax.rsqrt(var + LN_EPS) * lg_ref[...] + lb_ref[...]
    oc = _dot((yn * _sigmoid(yn)).astype(BF16), wc_ref[...])
    pa = _dot(oa_ref[0], wa_ref[...])
    pb = _dot(ob_ref[0], wb_ref[...])
    merged = (gate_ref[0, :, 0:d].astype(F32) * pa
              + gate_ref[0, :, d:2 * d].astype(F32) * pb
              + gate_ref[0, :, 2 * d:3 * d].astype(F32) * oc)
    o_ref[0] = x_ref[0] + _dot(merged.astype(BF16), wo_ref[...])


def _merge(u, hist, oa, ob, gate, x, cw, cb, lg, lb, wc, wa, wb, wo, *, tm, hist_from_u):
    n, t, c = u.shape
    d = x.shape[2]
    tile = lambda ch: pl.BlockSpec((1, tm, ch), lambda b, i: (b, i, 0))
    if hist_from_u:
        per = tm // CONV_HIST
        hspec = pl.BlockSpec((1, CONV_HIST, c), lambda b, i: (b, jnp.maximum(i * per - 1, 0), 0))
    else:
        hspec = pl.BlockSpec((1, CONV_HIST, c), lambda b, i: (b, 0, 0))
    vec = lambda a: a.reshape(1, -1)
    return pl.pallas_call(
        functools.partial(_merge_body, zero_first_hist=hist_from_u),
        out_shape=jax.ShapeDtypeStruct((n, t, d), F32),
        grid=(n, t // tm),
        in_specs=[tile(c), hspec, tile(oa.shape[2]), tile(ob.shape[2]), tile(3 * d), tile(d),
                  _resident(cw.shape), _resident((1, c)), _resident((1, c)), _resident((1, c)),
                  _resident(wc.shape), _resident(wa.shape), _resident(wb.shape), _resident(wo.shape)],
        out_specs=tile(d),
        scratch_shapes=[pltpu.VMEM((CONV_HIST + tm, c), F32)],
        compiler_params=_params(2),
        name="conv_merge",
    )(u, hist, oa, ob, gate, x, cw, vec(cb), vec(lg), vec(lb), wc, wa, wb, wo)


def _pad_rows(a, rows):
    return jnp.pad(a, ((0, 0), (0, rows - a.shape[1]), (0, 0)))


def kernel(x_prompt, x_sample, cache_a_k, cache_a_v, cache_b_k, cache_b_v, state_conv, w_in, b_gate,
           rel_bias, w_a_out, w_b_out, conv_w, conv_b, conv_ln_g, conv_ln_b, w_c_out, w_o, ln_ffn1,
           ffn1_w_gate, ffn1_w_up, ffn1_w_down, ln_mix, ln_ffn2, ffn2_w_gate, ffn2_w_up, ffn2_w_down,
           final_norm):
    depth = w_in.shape[0]
    nb, t, d = x_prompt.shape
    ns, ts, _ = x_sample.shape
    a_len = cache_a_k.shape[2]
    past = cache_b_k.shape[2]
    h_a, h_b = cache_a_k.shape[3], cache_b_k.shape[3]
    wa_width, wb_width = h_a * HEAD_DIM, h_b * HEAD_DIM
    c_conv = state_conv.shape[3]
    assert wa_width == wb_width == c_conv
    assert t % 512 == 0 and a_len % BAND_BLOCK == 0 and past % SB_BLOCK == 0
    assert a_len == A_PAST_CHUNKS * CHUNK and past % CHUNK == 0 and ts <= CHUNK and ts % 16 == 0
    assert t >= a_len

    bf = lambda a: a.astype(BF16)
    w_in16, wa16, wb16, wc16, wo16 = bf(w_in), bf(w_a_out), bf(w_b_out), bf(w_c_out), bf(w_o)
    f1g, f1u, f1d = bf(ffn1_w_gate), bf(ffn1_w_up), bf(ffn1_w_down)
    f2g, f2u, f2d = bf(ffn2_w_gate), bf(ffn2_w_up), bf(ffn2_w_down)

    bias_p = _band_bias(rel_bias)
    bias_s = _band_bias(rel_bias, kv_len=a_len + ts)
    band_qoff_s = a_len // BAND_BLOCK
    sb_qoff_s = past // SB_BLOCK

    xp = x_prompt.reshape(nb * t, d)
    xs = x_sample.reshape(ns * ts, d)
    p_state, s_state = [], []
    for l in range(depth):
        last = l == depth - 1
        mix_w = (conv_w[l], conv_b[l], conv_ln_g[l], conv_ln_b[l], wc16[l], wa16[l], wb16[l], wo16[l])

        xp = _ffn(xp, ln_ffn1[l], f1g[l], f1u[l], f1d[l], final_norm, final=False)
        (qa, ka32, va32, ka16, va16, qb, kb32, vb32, kb16, vb16, u, gate) = _inproj(
            xp, ln_mix[l], w_in16[l], b_gate[l], width=wa_width)
        seq = lambda a: a.reshape(nb, t, a.shape[-1])
        oa = _band_attention(seq(qa), seq(ka16), seq(va16), bias_p[l], qoff=0)
        ob = _stick_breaking(seq(qb), seq(kb16), seq(vb16), qoff=0)
        u3 = seq(u)
        xp = _merge(u3, u3, oa, ob, seq(gate), seq(xp), *mix_w, tm=512, hist_from_u=True)
        xp = _ffn(xp.reshape(nb * t, d), ln_ffn2[l], f2g[l], f2u[l], f2d[l], final_norm, final=last)
        heads = lambda a, h: a.reshape(nb, t, h, HEAD_DIM)
        p_state.append((heads(ka32, h_a)[:, t - a_len:], heads(va32, h_a)[:, t - a_len:],
                        heads(kb32, h_b), heads(vb32, h_b), u3[:, t - (CONV_W - 1):]))

        xs = _ffn(xs, ln_ffn1[l], f1g[l], f1u[l], f1d[l], final_norm, final=False)
        (qa, ka32, va32, ka16, va16, qb, kb32, vb32, kb16, vb16, u, gate) = _inproj(
            xs, ln_mix[l], w_in16[l], b_gate[l], width=wa_width)
        seq = lambda a: a.reshape(ns, ts, a.shape[-1])
        a_rows = a_len + BAND_BLOCK
        cat_a = lambda cache, new: _pad_rows(
            jnp.concatenate([bf(cache.reshape(ns, a_len, wa_width)), seq(new)], axis=1), a_rows)
        oa = _band_attention(_pad_rows(seq(qa), BAND_BLOCK), cat_a(cache_a_k[l], ka16),
                             cat_a(cache_a_v[l], va16), bias_s[l], qoff=band_qoff_s)[:, :ts]
        b_rows = past + SB_BLOCK
        cat_b = lambda cache, new: _pad_rows(
            jnp.concatenate([bf(cache.reshape(ns, past, wb_width)), seq(new)], axis=1), b_rows)
        ob = _stick_breaking(_pad_rows(seq(qb), SB_BLOCK), cat_b(cache_b_k[l], kb16),
                             cat_b(cache_b_v[l], vb16), qoff=sb_qoff_s)[:, :ts]
        u3 = seq(u)
        hist = jnp.pad(state_conv[l], ((0, 0), (CONV_HIST - (CONV_W - 1), 0), (0, 0)))
        xs = _merge(u3, hist, oa, ob, seq(gate), seq(xs), *mix_w, tm=ts, hist_from_u=False)
        xs = _ffn(xs.reshape(ns * ts, d), ln_ffn2[l], f2g[l], f2u[l], f2d[l], final_norm, final=last)
        heads = lambda a, h: a.reshape(ns, ts, h, HEAD_DIM)
        up = jnp.concatenate([state_conv[l], u3], axis=1)
        s_state.append((heads(ka32, h_a), heads(va32, h_a), heads(kb32, h_b), heads(vb32, h_b),
                        up[:, ts:]))

    stack = lambda states, i: jnp.stack([s[i] for s in states], axis=0)
    return (xp.reshape(nb, t, d), xs.reshape(ns, ts, d),
            stack(p_state, 0), stack(p_state, 1), stack(p_state, 2), stack(p_state, 3), stack(p_state, 4),
            stack(s_state, 0), stack(s_state, 1), stack(s_state, 2), stack(s_state, 3), stack(s_state, 4))
```

```python
import functools

import numpy as np
import jax
import jax.numpy as jnp
from jax import lax
from jax.experimental import pallas as pl
from jax.experimental.pallas import tpu as pltpu

F32 = jnp.float32
BF16 = jnp.bfloat16

HEAD_DIM = 64
CHUNK = 64
A_PAST_CHUNKS = 8
REL_MIN = -(CHUNK - 1)
REL_MAX = 128
CONV_W = 31
RMS_EPS = 1e-6
LN_EPS = 1e-5
NEG_INF = -1e30
LOG2_E = 1.4426950408889634
Q_SCALE = HEAD_DIM ** -0.5 * LOG2_E

LANES = 128
SUBLANES = 8
BAND_BLOCK = 2 * CHUNK
BAND_NBLK = A_PAST_CHUNKS * CHUNK // BAND_BLOCK + 1
SB_BLOCK = 256
SB_HEADS = 4
SB_DEAD = -175.0
CONV_HIST = 32
VMEM_LIMIT = 56 * 1024 * 1024


def _params(n_axes, vmem=VMEM_LIMIT):
    return pltpu.CompilerParams(dimension_semantics=("arbitrary",) * n_axes, vmem_limit_bytes=vmem)


def _resident(shape):
    nd = len(shape)
    return pl.BlockSpec(shape, lambda *_: (0,) * nd, pipeline_mode=pl.Buffered(1))


def _row_tile(m, want):
    tm = want
    while tm > 16 and m % tm:
        tm //= 2
    assert m % tm == 0
    return tm


def _rms(x, g):
    return x * lax.rsqrt(jnp.mean(x * x, axis=-1, keepdims=True) + RMS_EPS) * g


def _sigmoid(x):
    return 1.0 / (1.0 + jnp.exp(-x))


def _dot(a, b):
    return jnp.dot(a, b, preferred_element_type=F32)


def _dot_nt(a, b):
    return lax.dot_general(a, b, (((1,), (1,)), ((), ())), preferred_element_type=F32)


def _ffn_body(x_ref, g_ref, wg_ref, wu_ref, wd_ref, fg_ref, o_ref, *, tf, final):
    x = x_ref[...]
    h = _rms(x, g_ref[...]).astype(BF16)
    acc = None
    for c in range(wg_ref.shape[1] // tf):
        a = _dot(h, wg_ref[:, c * tf:(c + 1) * tf])
        b = _dot(h, wu_ref[:, c * tf:(c + 1) * tf])
        act = (a * _sigmoid(a) * b).astype(BF16)
        d = _dot(act, wd_ref[c * tf:(c + 1) * tf, :])
        acc = d if acc is None else acc + d
    y = x + 0.5 * acc
    if final:
        y = _rms(y, fg_ref[...])
    o_ref[...] = y


def _ffn(x, g, wg, wu, wd, fg, *, final):
    m, d = x.shape
    d_ff = wg.shape[1]
    tm = _row_tile(m, 512)
    tf = d_ff // 2 if (d_ff // 2) % LANES == 0 else d_ff
    row = pl.BlockSpec((tm, d), lambda i: (i, 0))
    return pl.pallas_call(
        functools.partial(_ffn_body, tf=tf, final=final),
        out_shape=jax.ShapeDtypeStruct((m, d), F32),
        grid=(m // tm,),
        in_specs=[row, _resident((1, d)), _resident((d, d_ff)), _resident((d, d_ff)),
                  _resident((d_ff, d)), _resident((1, d))],
        out_specs=row,
        compiler_params=_params(1),
        name="ffn_half",
    )(x, g.reshape(1, d), wg, wu, wd, fg.reshape(1, d))


def _inproj_body(x_ref, g_ref, w_ref, bg_ref, qa_ref, ka32_ref, va32_ref, ka16_ref, va16_ref,
                 qb_ref, kb32_ref, vb32_ref, kb16_ref, vb16_ref, u_ref, gate_ref, *, width, d_model):
    h = _rms(x_ref[...], g_ref[...]).astype(BF16)

    def mm(col, n):
        return _dot(h, w_ref[:, col:col + n])

    scale = Q_SCALE
    col = 0
    for q_ref, k32_ref, v32_ref, k16_ref, v16_ref in (
            (qa_ref, ka32_ref, va32_ref, ka16_ref, va16_ref),
            (qb_ref, kb32_ref, vb32_ref, kb16_ref, vb16_ref)):
        q_ref[...] = (mm(col, width) * scale).astype(BF16)
        k = mm(col + width, width)
        k32_ref[...] = k
        k16_ref[...] = k.astype(BF16)
        v = mm(col + 2 * width, width)
        v32_ref[...] = v
        v16_ref[...] = v.astype(BF16)
        col += 3 * width
    lin = mm(col, width)
    gt = mm(col + width, width)
    u_ref[...] = lin * _sigmoid(gt)
    col += 2 * width
    for j in range(3):
        graw = mm(col + j * d_model, d_model) + bg_ref[:, j * d_model:(j + 1) * d_model]
        gate_ref[:, j * d_model:(j + 1) * d_model] = _sigmoid(graw).astype(BF16)


def _inproj(x, g, w_in, b_gate, *, width):
    m, d = x.shape
    n_cols = w_in.shape[1]
    tm = _row_tile(m, 256)
    row = lambda n: pl.BlockSpec((tm, n), lambda i: (i, 0))
    sds = lambda n, dt: jax.ShapeDtypeStruct((m, n), dt)
    mixer = [sds(width, BF16), sds(width, F32), sds(width, F32), sds(width, BF16), sds(width, BF16)]
    mixer_specs = [row(width)] * 5
    return pl.pallas_call(
        functools.partial(_inproj_body, width=width, d_model=d),
        out_shape=mixer + mixer + [sds(width, F32), sds(3 * d, BF16)],
        grid=(m // tm,),
        in_specs=[row(d), _resident((1, d)), _resident((d, n_cols)), _resident((1, 3 * d))],
        out_specs=mixer_specs + mixer_specs + [row(width), row(3 * d)],
        compiler_params=_params(1),
        name="in_projection",
    )(x, g.reshape(1, d), w_in, b_gate.reshape(1, 3 * d))


def _band_body(q_ref, k_ref, v_ref, bias_ref, o_ref, *, qoff, n_heads):
    blk = pl.program_id(1) + qoff
    low = lax.broadcasted_iota(jnp.int32, (BAND_BLOCK, LANES), 1) < HEAD_DIM
    first = [blk + j - (BAND_NBLK - 1) for j in range(BAND_NBLK)]
    starts = [pl.multiple_of(jnp.maximum(f, 0) * BAND_BLOCK, BAND_BLOCK) for f in first]
    for hp in range(n_heads // 2):
        lanes = slice(hp * LANES, (hp + 1) * LANES)
        q = q_ref[0, :, lanes].astype(F32)
        q2 = jnp.concatenate([jnp.where(low, q, 0.0), jnp.where(low, 0.0, q)], axis=0).astype(BF16)
        s = []
        for j in range(BAND_NBLK):
            sj = _dot_nt(q2, k_ref[0, pl.ds(starts[j], BAND_BLOCK), lanes]) + bias_ref[hp, j]
            if j < BAND_NBLK - 1:
                sj = jnp.where(first[j] >= 0, sj, NEG_INF)
            s.append(sj)
        mx = functools.reduce(jnp.maximum, [jnp.max(sj, axis=-1, keepdims=True) for sj in s])
        p = [jnp.exp2(sj - mx) for sj in s]
        den = functools.reduce(jnp.add, [jnp.sum(pj, axis=-1, keepdims=True) for pj in p])
        o = functools.reduce(jnp.add, [
            _dot(pj.astype(BF16), v_ref[0, pl.ds(starts[j], BAND_BLOCK), lanes])
            for j, pj in enumerate(p)])
        o = o * (1.0 / den)
        o_ref[0, :, lanes] = jnp.where(low, o[:BAND_BLOCK], o[BAND_BLOCK:]).astype(BF16)


def _band_attention(q, k, v, bias, *, qoff):
    n, tq, w = q.shape
    tk = k.shape[1]
    n_heads = w // HEAD_DIM
    qspec = pl.BlockSpec((1, BAND_BLOCK, w), lambda b, i: (b, i, 0))
    kspec = pl.BlockSpec((1, tk, w), lambda b, i: (b, 0, 0), pipeline_mode=pl.Buffered(1))
    return pl.pallas_call(
        functools.partial(_band_body, qoff=qoff, n_heads=n_heads),
        out_shape=jax.ShapeDtypeStruct((n, tq, w), BF16),
        grid=(n, tq // BAND_BLOCK),
        in_specs=[qspec, kspec, kspec, _resident(bias.shape)],
        out_specs=qspec,
        compiler_params=_params(2),
        name="band_attention",
    )(q, k, v, bias)


def _band_bias(rel_bias, kv_len):
    nb = BAND_BLOCK
    r = np.arange(nb)[:, None]
    c = np.arange(nb)[None, :]
    ok = np.zeros((BAND_NBLK, nb, nb), bool)
    real = np.zeros((BAND_NBLK, nb, nb), bool)
    m = np.arange(2 * nb)
    diff = np.where(m < nb, -m, 2 * nb - m)
    line_idx = np.zeros((BAND_NBLK, 2 * nb), np.int32)
    for j in range(BAND_NBLK):
        line_idx[j] = np.clip(diff + nb * (BAND_NBLK - 1 - j), REL_MIN, REL_MAX) - REL_MIN
        dchunk = (A_PAST_CHUNKS + r // CHUNK) - (c // CHUNK + (nb // CHUNK) * j)
        ok[j] = (dchunk >= 0) & (dchunk <= A_PAST_CHUNKS)
        real[j] = np.broadcast_to((nb * j + c) < kv_len, (nb, nb))
    depth, heads = rel_bias.shape[:2]
    line = (rel_bias.astype(F32) * LOG2_E)[:, :, line_idx]
    line = line.reshape(depth, heads // 2, 2, BAND_NBLK, 2 * nb).transpose(0, 1, 3, 2, 4)
    table = jnp.tile(line, (1, 1, 1, 1, nb))[..., :nb * (2 * nb - 1)]
    table = table.reshape(depth, heads // 2, BAND_NBLK, 2, nb, 2 * nb - 1)[..., :nb]
    table = jnp.where(ok[None, None, :, None], table, NEG_INF)
    padded = jnp.where(real[None, None, :, None], table, NEG_INF)
    shape = (depth, heads // 2, BAND_NBLK, 2 * nb, nb)
    return table.reshape(shape), padded.reshape(shape)


def _sb_body(q_ref, k_ref, v_ref, tri_ref, o_ref, *, qoff):
    blk = pl.program_id(2) + qoff
    width = SB_HEADS * HEAD_DIM
    rows = SB_HEADS * SB_BLOCK
    lane = lax.broadcasted_iota(jnp.int32, (SB_BLOCK, width), 1)
    q = q_ref[0].astype(F32)
    qs = jnp.concatenate(
        [jnp.where((lane >= h * HEAD_DIM) & (lane < (h + 1) * HEAD_DIM), q, 0.0)
         for h in range(SB_HEADS)], axis=0).astype(BF16)
    tri = tri_ref[...]
    row = lax.broadcasted_iota(jnp.int32, (rows, SB_BLOCK), 0) & (SB_BLOCK - 1)
    col = lax.broadcasted_iota(jnp.int32, (rows, SB_BLOCK), 1)
    causal = col < row
    low = lax.broadcasted_iota(jnp.int32, (SB_BLOCK, LANES), 1) < HEAD_DIM

    def sweep(kb, acc, later, diagonal):
        start = pl.multiple_of(kb * SB_BLOCK, SB_BLOCK)
        k = k_ref[0, pl.ds(start, SB_BLOCK), :]
        v = v_ref[0, pl.ds(start, SB_BLOCK), :]
        z = _dot_nt(qs, k)
        drop = jnp.maximum(z, 0.0) + jnp.log2(1.0 + jnp.exp2(-jnp.abs(z)))
        if diagonal:
            drop = jnp.where(causal, drop, 0.0)
        suffix = _dot(drop.astype(BF16), tri)
        a = jnp.exp2(z + suffix + later)
        if diagonal:
            a = jnp.where(causal, a, 0.0)
        pv = _dot(a.astype(BF16), v)
        pieces = [pv[h * SB_BLOCK:(h + 1) * SB_BLOCK, (h // 2) * LANES:(h // 2 + 1) * LANES]
                  for h in range(SB_HEADS)]
        picked = jnp.concatenate([jnp.where(low, pieces[0], pieces[1]),
                                  jnp.where(low, pieces[2], pieces[3])], axis=1)
        return acc + picked, later + suffix[:, 0:1]

    acc = jnp.zeros((SB_BLOCK, width), F32)
    later = jnp.zeros((rows, 1), F32)
    acc, later = sweep(blk, acc, later, True)

    def body(carry):
        _, kb, acc, later = carry
        acc, later = sweep(kb, acc, later, False)
        alive = jnp.max(later) > SB_DEAD
        more = jnp.logical_and(kb > 0, alive).astype(jnp.int32)
        return more, kb - 1, acc, later

    start = (blk > 0).astype(jnp.int32)
    _, _, acc, later = lax.while_loop(lambda c: c[0] > 0, body, (start, blk - 1, acc, later))
    o_ref[0] = acc.astype(BF16)


def _stick_breaking(q, k, v, *, qoff):
    n, tq, w = q.shape
    tk = k.shape[1]
    gw = SB_HEADS * HEAD_DIM
    tri = jnp.asarray(-np.tril(np.ones((SB_BLOCK, SB_BLOCK), np.float32)), BF16)
    qspec = pl.BlockSpec((1, SB_BLOCK, gw), lambda b, g, i: (b, i, g))
    kspec = pl.BlockSpec((1, tk, gw), lambda b, g, i: (b, 0, g))
    return pl.pallas_call(
        functools.partial(_sb_body, qoff=qoff),
        out_shape=jax.ShapeDtypeStruct((n, tq, w), BF16),
        grid=(n, w // gw, tq // SB_BLOCK),
        in_specs=[qspec, kspec, kspec, _resident((SB_BLOCK, SB_BLOCK))],
        out_specs=qspec,
        compiler_params=_params(3),
        name="stick_breaking",
    )(q, k, v, tri)


def _merge_body(u_ref, hist_ref, oa_ref, ob_ref, gate_ref, x_ref, cw_ref, cb_ref, lg_ref, lb_ref,
                wc_ref, wa_ref, wb_ref, wo_ref, o_ref, up_ref, rot_ref, *, zero_first_hist):
    tm = u_ref.shape[1]
    d = x_ref.shape[2]
    hist = hist_ref[0]
    if zero_first_hist:
        hist = jnp.where(pl.program_id(1) > 0, hist, 0.0)
    up_ref[0:CONV_HIST, :] = hist
    up_ref[CONV_HIST:CONV_HIST + tm, :] = u_ref[0]
    span = rot_ref.shape[1]
    for b in range(1, SUBLANES):
        rot_ref[b - 1] = up_ref[b:b + span, :]
    first = CONV_HIST - (CONV_W - 1)
    y = None
    for w in range(CONV_W):
        a, b = divmod(first + w, SUBLANES)
        if b == 0:
            window = up_ref[SUBLANES * a:SUBLANES * a + tm, :]
        else:
            window = rot_ref[b - 1, SUBLANES * a:SUBLANES * a + tm, :]
        term = window * cw_ref[w:w + 1, :]
        y = term if y is None else y + term
    y = y + cb_ref[...]
    mu = jnp.mean(y, axis=-1, keepdims=True)
    yc = y - mu
    var = jnp.mean(yc * yc, axis=-1, keepdims=True)
    yn = yc * lax.rsqrt(var + LN_EPS) * lg_ref[...] + lb_ref[...]
    oc = _dot((yn * _sigmoid(yn)).astype(BF16), wc_ref[...])
    pa = _dot(oa_ref[0], wa_ref[...])
    pb = _dot(ob_ref[0], wb_ref[...])
    merged = (gate_ref[0, :, 0:d].astype(F32) * pa
              + gate_ref[0, :, d:2 * d].astype(F32) * pb
              + gate_ref[0, :, 2 * d:3 * d].astype(F32) * oc)
    o_ref[0] = x_ref[0] + _dot(merged.astype(BF16), wo_ref[...])


def _merge(u, hist, oa, ob, gate, x, cw, cb, lg, lb, wc, wa, wb, wo, *, tm, hist_from_u):
    n, t, c = u.shape
    d = x.shape[2]
    tile = lambda ch: pl.BlockSpec((1, tm, ch), lambda b, i: (b, i, 0))
    if hist_from_u:
        per = tm // CONV_HIST
        hspec = pl.BlockSpec((1, CONV_HIST, c), lambda b, i: (b, jnp.maximum(i * per - 1, 0), 0))
    else:
        hspec = pl.BlockSpec((1, CONV_HIST, c), lambda b, i: (b, 0, 0))
    vec = lambda a: a.reshape(1, -1)
    return pl.pallas_call(
        functools.partial(_merge_body, zero_first_hist=hist_from_u),
        out_shape=jax.ShapeDtypeStruct((n, t, d), F32),
        grid=(n, t // tm),
        in_specs=[tile(c), hspec, tile(oa.shape[2]), tile(ob.shape[2]), tile(3 * d), tile(d),
                  _resident(cw.shape), _resident((1, c)), _resident((1, c)), _resident((1, c)),
                  _resident(wc.shape), _resident(wa.shape), _resident(wb.shape), _resident(wo.shape)],
        out_specs=tile(d),
        scratch_shapes=[pltpu.VMEM((CONV_HIST + tm, c), F32),
                        pltpu.VMEM((SUBLANES - 1, CONV_HIST + tm - SUBLANES, c), F32)],
        compiler_params=_params(2),
        name="conv_merge",
    )(u, hist, oa, ob, gate, x, cw, vec(cb), vec(lg), vec(lb), wc, wa, wb, wo)


def _pad_rows(a, rows):
    return jnp.pad(a, ((0, 0), (0, rows - a.shape[1]), (0, 0)))


def kernel(x_prompt, x_sample, cache_a_k, cache_a_v, cache_b_k, cache_b_v, state_conv, w_in, b_gate,
           rel_bias, w_a_out, w_b_out, conv_w, conv_b, conv_ln_g, conv_ln_b, w_c_out, w_o, ln_ffn1,
           ffn1_w_gate, ffn1_w_up, ffn1_w_down, ln_mix, ln_ffn2, ffn2_w_gate, ffn2_w_up, ffn2_w_down,
           final_norm):
    depth = w_in.shape[0]
    nb, t, d = x_prompt.shape
    ns, ts, _ = x_sample.shape
    a_len = cache_a_k.shape[2]
    past = cache_b_k.shape[2]
    h_a, h_b = cache_a_k.shape[3], cache_b_k.shape[3]
    wa_width, wb_width = h_a * HEAD_DIM, h_b * HEAD_DIM
    c_conv = state_conv.shape[3]
    assert wa_width == wb_width == c_conv
    assert t % 512 == 0 and a_len % BAND_BLOCK == 0 and past % SB_BLOCK == 0
    assert a_len == A_PAST_CHUNKS * CHUNK and past % CHUNK == 0 and ts <= CHUNK and ts % 16 == 0
    assert t >= a_len

    bf = lambda a: a.astype(BF16)
    w_in16, wa16, wb16, wc16, wo16 = bf(w_in), bf(w_a_out), bf(w_b_out), bf(w_c_out), bf(w_o)
    f1g, f1u, f1d = bf(ffn1_w_gate), bf(ffn1_w_up), bf(ffn1_w_down)
    f2g, f2u, f2d = bf(ffn2_w_gate), bf(ffn2_w_up), bf(ffn2_w_down)

    bias_p, bias_s = _band_bias(rel_bias, kv_len=a_len + ts)
    band_qoff_s = a_len // BAND_BLOCK
    sb_qoff_s = past // SB_BLOCK

    xp = x_prompt.reshape(nb * t, d)
    xs = x_sample.reshape(ns * ts, d)
    p_state, s_state = [], []
    for l in range(depth):
        last = l == depth - 1
        mix_w = (conv_w[l], conv_b[l], conv_ln_g[l], conv_ln_b[l], wc16[l], wa16[l], wb16[l], wo16[l])

        xp = _ffn(xp, ln_ffn1[l], f1g[l], f1u[l], f1d[l], final_norm, final=False)
        (qa, ka32, va32, ka16, va16, qb, kb32, vb32, kb16, vb16, u, gate) = _inproj(
            xp, ln_mix[l], w_in16[l], b_gate[l], width=wa_width)
        seq = lambda a: a.reshape(nb, t, a.shape[-1])
        oa = _band_attention(seq(qa), seq(ka16), seq(va16), bias_p[l], qoff=0)
        ob = _stick_breaking(seq(qb), seq(kb16), seq(vb16), qoff=0)
        u3 = seq(u)
        xp = _merge(u3, u3, oa, ob, seq(gate), seq(xp), *mix_w, tm=512, hist_from_u=True)
        xp = _ffn(xp.reshape(nb * t, d), ln_ffn2[l], f2g[l], f2u[l], f2d[l], final_norm, final=last)
        heads = lambda a, h: a.reshape(nb, t, h, HEAD_DIM)
        tail = lambda a, h: seq(a)[:, t - a_len:].reshape(nb, a_len, h, HEAD_DIM)
        p_state.append((tail(ka32, h_a), tail(va32, h_a),
                        heads(kb32, h_b), heads(vb32, h_b), u3[:, t - (CONV_W - 1):]))

        xs = _ffn(xs, ln_ffn1[l], f1g[l], f1u[l], f1d[l], final_norm, final=False)
        (qa, ka32, va32, ka16, va16, qb, kb32, vb32, kb16, vb16, u, gate) = _inproj(
            xs, ln_mix[l], w_in16[l], b_gate[l], width=wa_width)
        seq = lambda a: a.reshape(ns, ts, a.shape[-1])
        a_rows = a_len + BAND_BLOCK
        cat_a = lambda cache, new: _pad_rows(
            jnp.concatenate([bf(cache.reshape(ns, a_len, wa_width)), seq(new)], axis=1), a_rows)
        oa = _band_attention(_pad_rows(seq(qa), BAND_BLOCK), cat_a(cache_a_k[l], ka16),
                             cat_a(cache_a_v[l], va16), bias_s[l], qoff=band_qoff_s)[:, :ts]
        b_rows = past + SB_BLOCK
        cat_b = lambda cache, new: _pad_rows(
            jnp.concatenate([bf(cache.reshape(ns, past, wb_width)), seq(new)], axis=1), b_rows)
        ob = _stick_breaking(_pad_rows(seq(qb), SB_BLOCK), cat_b(cache_b_k[l], kb16),
                             cat_b(cache_b_v[l], vb16), qoff=sb_qoff_s)[:, :ts]
        u3 = seq(u)
        hist = jnp.pad(state_conv[l], ((0, 0), (CONV_HIST - (CONV_W - 1), 0), (0, 0)))
        xs = _merge(u3, hist, oa, ob, seq(gate), seq(xs), *mix_w, tm=ts, hist_from_u=False)
        xs = _ffn(xs.reshape(ns * ts, d), ln_ffn2[l], f2g[l], f2u[l], f2d[l], final_norm, final=last)
        heads = lambda a, h: a.reshape(ns, ts, h, HEAD_DIM)
        up = jnp.concatenate([state_conv[l], u3], axis=1)
        s_state.append((heads(ka32, h_a), heads(va32, h_a), heads(kb32, h_b), heads(vb32, h_b),
                        up[:, ts:]))

    stack = lambda states, i: jnp.stack([s[i] for s in states], axis=0)
    return (xp.reshape(nb, t, d), xs.reshape(ns, ts, d),
            stack(p_state, 0), stack(p_state, 1), stack(p_state, 2), stack(p_state, 3), stack(p_state, 4),
            stack(s_state, 0), stack(s_state, 1), stack(s_state, 2), stack(s_state, 3), stack(s_state, 4))
```

```python
import functools

import numpy as np
import jax
import jax.numpy as jnp
from jax import lax
from jax.experimental import pallas as pl
from jax.experimental.pallas import tpu as pltpu

F32 = jnp.float32
BF16 = jnp.bfloat16

HEAD_DIM = 64
CHUNK = 64
A_PAST_CHUNKS = 8
REL_MIN = -(CHUNK - 1)
REL_MAX = 128
CONV_W = 31
RMS_EPS = 1e-6
LN_EPS = 1e-5
NEG_INF = -1e30
LOG2_E = 1.4426950408889634
Q_SCALE = HEAD_DIM ** -0.5 * LOG2_E

LANES = 128
SUBLANES = 8
BAND_BLOCK = 2 * CHUNK
BAND_NBLK = A_PAST_CHUNKS * CHUNK // BAND_BLOCK + 1
SB_BLOCK = 256
SB_HEADS = 4
SB_DEAD = -175.0
CONV_HIST = 32
VMEM_LIMIT = 56 * 1024 * 1024


def _params(n_axes, vmem=VMEM_LIMIT):
    return pltpu.CompilerParams(dimension_semantics=("arbitrary",) * n_axes, vmem_limit_bytes=vmem)


def _resident(shape):
    nd = len(shape)
    return pl.BlockSpec(shape, lambda *_: (0,) * nd, pipeline_mode=pl.Buffered(1))


def _row_tile(m, want):
    tm = want
    while tm > 16 and m % tm:
        tm //= 2
    assert m % tm == 0
    return tm


def _rms(x, g):
    return x * lax.rsqrt(jnp.mean(x * x, axis=-1, keepdims=True) + RMS_EPS) * g


def _sigmoid(x):
    return 1.0 / (1.0 + jnp.exp(-x))


def _dot(a, b):
    return jnp.dot(a, b, preferred_element_type=F32)


def _dot_nt(a, b):
    return lax.dot_general(a, b, (((1,), (1,)), ((), ())), preferred_element_type=F32)


CAST_BLOCK_BYTES = 4 * 1024 * 1024


def _cast_body(x_ref, o_ref):
    o_ref[...] = x_ref[...].astype(BF16)


def _to_bf16(w):
    cols = w.shape[-1]
    w2 = w.reshape(-1, cols)
    rows = w2.shape[0]
    want = 1024
    while want * cols * 4 > CAST_BLOCK_BYTES:
        want //= 2
    tr = _row_tile(rows, want)
    spec = pl.BlockSpec((tr, cols), lambda i: (i, 0))
    out = pl.pallas_call(
        _cast_body,
        out_shape=jax.ShapeDtypeStruct((rows, cols), BF16),
        grid=(rows // tr,),
        in_specs=[spec],
        out_specs=spec,
        compiler_params=_params(1),
        name="cast_bf16",
    )(w2)
    return out.reshape(w.shape)


def _ffn_body(x_ref, g_ref, wg_ref, wu_ref, wd_ref, fg_ref, o_ref, *, tf, final):
    x = x_ref[...]
    h = _rms(x, g_ref[...]).astype(BF16)
    acc = None
    for c in range(wg_ref.shape[1] // tf):
        a = _dot(h, wg_ref[:, c * tf:(c + 1) * tf])
        b = _dot(h, wu_ref[:, c * tf:(c + 1) * tf])
        act = (a * _sigmoid(a) * b).astype(BF16)
        d = _dot(act, wd_ref[c * tf:(c + 1) * tf, :])
        acc = d if acc is None else acc + d
    y = x + 0.5 * acc
    if final:
        y = _rms(y, fg_ref[...])
    o_ref[...] = y


def _ffn(x, g, wg, wu, wd, fg, *, final):
    m, d = x.shape
    d_ff = wg.shape[1]
    tm = _row_tile(m, 512)
    tf = d_ff // 2 if (d_ff // 2) % LANES == 0 else d_ff
    row = pl.BlockSpec((tm, d), lambda i: (i, 0))
    return pl.pallas_call(
        functools.partial(_ffn_body, tf=tf, final=final),
        out_shape=jax.ShapeDtypeStruct((m, d), F32),
        grid=(m // tm,),
        in_specs=[row, _resident((1, d)), _resident((d, d_ff)), _resident((d, d_ff)),
                  _resident((d_ff, d)), _resident((1, d))],
        out_specs=row,
        compiler_params=_params(1),
        name="ffn_half",
    )(x, g.reshape(1, d), wg, wu, wd, fg.reshape(1, d))


def _inproj_body(x_ref, g_ref, w_ref, bg_ref, qa_ref, ka32_ref, va32_ref, ka16_ref, va16_ref,
                 qb_ref, kb32_ref, vb32_ref, kb16_ref, vb16_ref, u_ref, gate_ref, *, width, d_model,
                 time_minor):
    h = _rms(x_ref[...], g_ref[...]).astype(BF16)

    def mm(col, n):
        return _dot(h, w_ref[:, col:col + n])

    def put_state(ref, val):
        if time_minor:
            ref[0] = val.T
        else:
            ref[...] = val

    scale = Q_SCALE
    col = 0
    for q_ref, k32_ref, v32_ref, k16_ref, v16_ref in (
            (qa_ref, ka32_ref, va32_ref, ka16_ref, va16_ref),
            (qb_ref, kb32_ref, vb32_ref, kb16_ref, vb16_ref)):
        q_ref[...] = (mm(col, width) * scale).astype(BF16)
        k = mm(col + width, width)
        put_state(k32_ref, k)
        k16_ref[...] = k.astype(BF16)
        v = mm(col + 2 * width, width)
        put_state(v32_ref, v)
        v16_ref[...] = v.astype(BF16)
        col += 3 * width
    lin = mm(col, width)
    gt = mm(col + width, width)
    u_ref[...] = lin * _sigmoid(gt)
    col += 2 * width
    for j in range(3):
        graw = mm(col + j * d_model, d_model) + bg_ref[:, j * d_model:(j + 1) * d_model]
        gate_ref[:, j * d_model:(j + 1) * d_model] = _sigmoid(graw).astype(BF16)


def _inproj(x, g, w_in, b_gate, *, width, seq_len=None):
    m, d = x.shape
    n_cols = w_in.shape[1]
    tm = _row_tile(m, 256)
    row = lambda n: pl.BlockSpec((tm, n), lambda i: (i, 0))
    sds = lambda n, dt: jax.ShapeDtypeStruct((m, n), dt)
    if seq_len is None:
        state, state_spec = sds(width, F32), row(width)
    else:
        per_seq = seq_len // tm
        state = jax.ShapeDtypeStruct((m // seq_len, width, seq_len), F32)
        state_spec = pl.BlockSpec((1, width, tm), lambda i: (i // per_seq, 0, i % per_seq))
    mixer = [sds(width, BF16), state, state, sds(width, BF16), sds(width, BF16)]
    mixer_specs = [row(width), state_spec, state_spec, row(width), row(width)]
    return pl.pallas_call(
        functools.partial(_inproj_body, width=width, d_model=d, time_minor=seq_len is not None),
        out_shape=mixer + mixer + [sds(width, F32), sds(3 * d, BF16)],
        grid=(m // tm,),
        in_specs=[row(d), _resident((1, d)), _resident((d, n_cols)), _resident((1, 3 * d))],
        out_specs=mixer_specs + mixer_specs + [row(width), row(3 * d)],
        compiler_params=_params(1),
        name="in_projection",
    )(x, g.reshape(1, d), w_in, b_gate.reshape(1, 3 * d))


def _band_body(q_ref, k_ref, v_ref, bias_ref, o_ref, *, qoff, n_heads):
    blk = pl.program_id(1) + qoff
    low = lax.broadcasted_iota(jnp.int32, (BAND_BLOCK, LANES), 1) < HEAD_DIM
    first = [blk + j - (BAND_NBLK - 1) for j in range(BAND_NBLK)]
    starts = [pl.multiple_of(jnp.maximum(f, 0) * BAND_BLOCK, BAND_BLOCK) for f in first]
    for hp in range(n_heads // 2):
        lanes = slice(hp * LANES, (hp + 1) * LANES)
        q = q_ref[0, :, lanes].astype(F32)
        q2 = jnp.concatenate([jnp.where(low, q, 0.0), jnp.where(low, 0.0, q)], axis=0).astype(BF16)
        s = []
        for j in range(BAND_NBLK):
            sj = _dot_nt(q2, k_ref[0, pl.ds(starts[j], BAND_BLOCK), lanes]) + bias_ref[hp, j]
            if j < BAND_NBLK - 1:
                sj = jnp.where(first[j] >= 0, sj, NEG_INF)
            s.append(sj)
        mx = functools.reduce(jnp.maximum, [jnp.max(sj, axis=-1, keepdims=True) for sj in s])
        p = [jnp.exp2(sj - mx) for sj in s]
        den = functools.reduce(jnp.add, [jnp.sum(pj, axis=-1, keepdims=True) for pj in p])
        o = functools.reduce(jnp.add, [
            _dot(pj.astype(BF16), v_ref[0, pl.ds(starts[j], BAND_BLOCK), lanes])
            for j, pj in enumerate(p)])
        o = o * (1.0 / den)
        o_ref[0, :, lanes] = jnp.where(low, o[:BAND_BLOCK], o[BAND_BLOCK:]).astype(BF16)


def _band_attention(q, k, v, bias, *, qoff):
    n, tq, w = q.shape
    tk = k.shape[1]
    n_heads = w // HEAD_DIM
    qspec = pl.BlockSpec((1, BAND_BLOCK, w), lambda b, i: (b, i, 0))
    kspec = pl.BlockSpec((1, tk, w), lambda b, i: (b, 0, 0), pipeline_mode=pl.Buffered(1))
    return pl.pallas_call(
        functools.partial(_band_body, qoff=qoff, n_heads=n_heads),
        out_shape=jax.ShapeDtypeStruct((n, tq, w), BF16),
        grid=(n, tq // BAND_BLOCK),
        in_specs=[qspec, kspec, kspec, _resident(bias.shape)],
        out_specs=qspec,
        compiler_params=_params(2),
        name="band_attention",
    )(q, k, v, bias)


def _band_bias(rel_bias, kv_len):
    nb = BAND_BLOCK
    r = np.arange(nb)[:, None]
    c = np.arange(nb)[None, :]
    ok = np.zeros((BAND_NBLK, nb, nb), bool)
    real = np.zeros((BAND_NBLK, nb, nb), bool)
    m = np.arange(2 * nb)
    diff = np.where(m < nb, -m, 2 * nb - m)
    line_idx = np.zeros((BAND_NBLK, 2 * nb), np.int32)
    for j in range(BAND_NBLK):
        line_idx[j] = np.clip(diff + nb * (BAND_NBLK - 1 - j), REL_MIN, REL_MAX) - REL_MIN
        dchunk = (A_PAST_CHUNKS + r // CHUNK) - (c // CHUNK + (nb // CHUNK) * j)
        ok[j] = (dchunk >= 0) & (dchunk <= A_PAST_CHUNKS)
        real[j] = np.broadcast_to((nb * j + c) < kv_len, (nb, nb))
    depth, heads = rel_bias.shape[:2]
    line = (rel_bias.astype(F32) * LOG2_E)[:, :, line_idx]
    line = line.reshape(depth, heads // 2, 2, BAND_NBLK, 2 * nb).transpose(0, 1, 3, 2, 4)
    where_m = lax.broadcasted_iota(jnp.int32, (2 * nb, nb, nb), 0)
    spread = (where_m == jnp.asarray((c - r) % (2 * nb), jnp.int32)[None]).astype(F32)
    table = jnp.einsum("dpjhm,mrc->dpjhrc", line, spread, precision=lax.Precision.HIGHEST)
    table = jnp.where(ok[None, None, :, None], table, NEG_INF)
    padded = jnp.where(real[None, None, :, None], table, NEG_INF)
    shape = (depth, heads // 2, BAND_NBLK, 2 * nb, nb)
    return table.reshape(shape), padded.reshape(shape)


def _sb_body(q_ref, k_ref, v_ref, tri_ref, o_ref, *, qoff):
    blk = pl.program_id(2) + qoff
    width = SB_HEADS * HEAD_DIM
    rows = SB_HEADS * SB_BLOCK
    lane = lax.broadcasted_iota(jnp.int32, (SB_BLOCK, width), 1)
    q = q_ref[0].astype(F32)
    qs = jnp.concatenate(
        [jnp.where((lane >= h * HEAD_DIM) & (lane < (h + 1) * HEAD_DIM), q, 0.0)
         for h in range(SB_HEADS)], axis=0).astype(BF16)
    tri = tri_ref[...]
    row = lax.broadcasted_iota(jnp.int32, (rows, SB_BLOCK), 0) & (SB_BLOCK - 1)
    col = lax.broadcasted_iota(jnp.int32, (rows, SB_BLOCK), 1)
    causal = col < row
    low = lax.broadcasted_iota(jnp.int32, (SB_BLOCK, LANES), 1) < HEAD_DIM

    def sweep(kb, acc, later, diagonal, valid=None):
        start = pl.multiple_of(kb * SB_BLOCK, SB_BLOCK)
        k = k_ref[0, pl.ds(start, SB_BLOCK), :]
        v = v_ref[0, pl.ds(start, SB_BLOCK), :]
        z = _dot_nt(qs, k)
        drop = jnp.maximum(z, 0.0) + jnp.log2(1.0 + jnp.exp2(-jnp.abs(z)))
        if diagonal:
            drop = jnp.where(causal, drop, 0.0)
        suffix = _dot(drop.astype(BF16), tri)
        a = jnp.exp2(z + suffix + later)
        if diagonal:
            a = jnp.where(causal, a, 0.0)
        pv = _dot(a.astype(BF16), v)
        pieces = [pv[h * SB_BLOCK:(h + 1) * SB_BLOCK, (h // 2) * LANES:(h // 2 + 1) * LANES]
                  for h in range(SB_HEADS)]
        picked = jnp.concatenate([jnp.where(low, pieces[0], pieces[1]),
                                  jnp.where(low, pieces[2], pieces[3])], axis=1)
        total = suffix[:, 0:1]
        if valid is not None:
            picked = jnp.where(valid, picked, 0.0)
            total = jnp.where(valid, total, 0.0)
        return acc + picked, later + total

    acc = jnp.zeros((SB_BLOCK, width), F32)
    later = jnp.zeros((rows, 1), F32)
    acc, later = sweep(blk, acc, later, True)
    acc, later = sweep(jnp.maximum(blk - 1, 0), acc, later, False, valid=blk > 0)

    def body(carry):
        _, kb, acc, later = carry
        acc, later = sweep(kb, acc, later, False)
        alive = jnp.max(later) > SB_DEAD
        more = jnp.logical_and(kb > 0, alive).astype(jnp.int32)
        return more, kb - 1, acc, later

    start = jnp.logical_and(blk > 1, jnp.max(later) > SB_DEAD).astype(jnp.int32)
    _, _, acc, later = lax.while_loop(lambda c: c[0] > 0, body, (start, blk - 2, acc, later))
    o_ref[0] = acc.astype(BF16)


def _stick_breaking(q, k, v, *, qoff):
    n, tq, w = q.shape
    tk = k.shape[1]
    gw = SB_HEADS * HEAD_DIM
    tri = jnp.asarray(-np.tril(np.ones((SB_BLOCK, SB_BLOCK), np.float32)), BF16)
    qspec = pl.BlockSpec((1, SB_BLOCK, gw), lambda b, g, i: (b, i, g))
    kspec = pl.BlockSpec((1, tk, gw), lambda b, g, i: (b, 0, g))
    return pl.pallas_call(
        functools.partial(_sb_body, qoff=qoff),
        out_shape=jax.ShapeDtypeStruct((n, tq, w), BF16),
        grid=(n, w // gw, tq // SB_BLOCK),
        in_specs=[qspec, kspec, kspec, _resident((SB_BLOCK, SB_BLOCK))],
        out_specs=qspec,
        compiler_params=_params(3),
        name="stick_breaking",
    )(q, k, v, tri)


def _merge_body(u_ref, hist_ref, oa_ref, ob_ref, gate_ref, x_ref, cw_ref, cb_ref, lg_ref, lb_ref,
                wc_ref, wa_ref, wb_ref, wo_ref, o_ref, up_ref, rot_ref, *, zero_first_hist):
    tm = u_ref.shape[1]
    d = x_ref.shape[2]
    hist = hist_ref[0]
    if zero_first_hist:
        hist = jnp.where(pl.program_id(1) > 0, hist, 0.0)
    up_ref[0:CONV_HIST, :] = hist
    up_ref[CONV_HIST:CONV_HIST + tm, :] = u_ref[0]
    span = rot_ref.shape[1]
    for b in range(1, SUBLANES):
        rot_ref[b - 1] = up_ref[b:b + span, :]
    first = CONV_HIST - (CONV_W - 1)
    y = None
    for w in range(CONV_W):
        a, b = divmod(first + w, SUBLANES)
        if b == 0:
            window = up_ref[SUBLANES * a:SUBLANES * a + tm, :]
        else:
            window = rot_ref[b - 1, SUBLANES * a:SUBLANES * a + tm, :]
        term = window * cw_ref[w:w + 1, :]
        y = term if y is None else y + term
    y = y + cb_ref[...]
    mu = jnp.mean(y, axis=-1, keepdims=True)
    yc = y - mu
    var = jnp.mean(yc * yc, axis=-1, keepdims=True)
    yn = yc * lax.rsqrt(var + LN_EPS) * lg_ref[...] + lb_ref[...]
    oc = _dot((yn * _sigmoid(yn)).astype(BF16), wc_ref[...])
    pa = _dot(oa_ref[0], wa_ref[...])
    pb = _dot(ob_ref[0], wb_ref[...])
    merged = (gate_ref[0, :, 0:d].astype(F32) * pa
              + gate_ref[0, :, d:2 * d].astype(F32) * pb
              + gate_ref[0, :, 2 * d:3 * d].astype(F32) * oc)
    o_ref[0] = x_ref[0] + _dot(merged.astype(BF16), wo_ref[...])


def _merge(u, hist, oa, ob, gate, x, cw, cb, lg, lb, wc, wa, wb, wo, *, tm, hist_from_u):
    n, t, c = u.shape
    d = x.shape[2]
    tile = lambda ch: pl.BlockSpec((1, tm, ch), lambda b, i: (b, i, 0))
    if hist_from_u:
        per = tm // CONV_HIST
        hspec = pl.BlockSpec((1, CONV_HIST, c), lambda b, i: (b, jnp.maximum(i * per - 1, 0), 0))
    else:
        hspec = pl.BlockSpec((1, CONV_HIST, c), lambda b, i: (b, 0, 0))
    vec = lambda a: a.reshape(1, -1)
    return pl.pallas_call(
        functools.partial(_merge_body, zero_first_hist=hist_from_u),
        out_shape=jax.ShapeDtypeStruct((n, t, d), F32),
        grid=(n, t // tm),
        in_specs=[tile(c), hspec, tile(oa.shape[2]), tile(ob.shape[2]), tile(3 * d), tile(d),
                  _resident(cw.shape), _resident((1, c)), _resident((1, c)), _resident((1, c)),
                  _resident(wc.shape), _resident(wa.shape), _resident(wb.shape), _resident(wo.shape)],
        out_specs=tile(d),
        scratch_shapes=[pltpu.VMEM((CONV_HIST + tm, c), F32),
                        pltpu.VMEM((SUBLANES - 1, CONV_HIST + tm - SUBLANES, c), F32)],
        compiler_params=_params(2),
        name="conv_merge",
    )(u, hist, oa, ob, gate, x, cw, vec(cb), vec(lg), vec(lb), wc, wa, wb, wo)


def _pad_rows(a, rows):
    return jnp.pad(a, ((0, 0), (0, rows - a.shape[1]), (0, 0)))


def kernel(x_prompt, x_sample, cache_a_k, cache_a_v, cache_b_k, cache_b_v, state_conv, w_in, b_gate,
           rel_bias, w_a_out, w_b_out, conv_w, conv_b, conv_ln_g, conv_ln_b, w_c_out, w_o, ln_ffn1,
           ffn1_w_gate, ffn1_w_up, ffn1_w_down, ln_mix, ln_ffn2, ffn2_w_gate, ffn2_w_up, ffn2_w_down,
           final_norm):
    depth = w_in.shape[0]
    nb, t, d = x_prompt.shape
    ns, ts, _ = x_sample.shape
    a_len = cache_a_k.shape[2]
    past = cache_b_k.shape[2]
    h_a, h_b = cache_a_k.shape[3], cache_b_k.shape[3]
    wa_width, wb_width = h_a * HEAD_DIM, h_b * HEAD_DIM
    c_conv = state_conv.shape[3]
    assert wa_width == wb_width == c_conv
    assert t % 512 == 0 and a_len % BAND_BLOCK == 0 and past % SB_BLOCK == 0
    assert a_len == A_PAST_CHUNKS * CHUNK and past % CHUNK == 0 and ts <= CHUNK and ts % 16 == 0
    assert t >= a_len

    bf = lambda a: a.astype(BF16)
    w_in16, wa16, wb16, wc16, wo16 = map(_to_bf16, (w_in, w_a_out, w_b_out, w_c_out, w_o))
    f1g, f1u, f1d = map(_to_bf16, (ffn1_w_gate, ffn1_w_up, ffn1_w_down))
    f2g, f2u, f2d = map(_to_bf16, (ffn2_w_gate, ffn2_w_up, ffn2_w_down))

    bias_p, bias_s = _band_bias(rel_bias, kv_len=a_len + ts)
    band_qoff_s = a_len // BAND_BLOCK
    sb_qoff_s = past // SB_BLOCK

    def padded_cache(cache, extra):
        flat = bf(cache.reshape(*cache.shape[:3], -1))
        return jnp.pad(flat, ((0, 0), (0, 0), (0, extra), (0, 0)))

    seq_a_k, seq_a_v = padded_cache(cache_a_k, BAND_BLOCK), padded_cache(cache_a_v, BAND_BLOCK)
    seq_b_k, seq_b_v = padded_cache(cache_b_k, SB_BLOCK), padded_cache(cache_b_v, SB_BLOCK)
    hist = jnp.pad(state_conv, ((0, 0), (0, 0), (CONV_HIST - (CONV_W - 1), 0), (0, 0)))

    xp = x_prompt.reshape(nb * t, d)
    xs = x_sample.reshape(ns * ts, d)
    p_state, s_state = [], []
    for l in range(depth):
        last = l == depth - 1
        mix_w = (conv_w[l], conv_b[l], conv_ln_g[l], conv_ln_b[l], wc16[l], wa16[l], wb16[l], wo16[l])

        xp = _ffn(xp, ln_ffn1[l], f1g[l], f1u[l], f1d[l], final_norm, final=False)
        (qa, ka32, va32, ka16, va16, qb, kb32, vb32, kb16, vb16, u, gate) = _inproj(
            xp, ln_mix[l], w_in16[l], b_gate[l], width=wa_width, seq_len=t)
        seq = lambda a: a.reshape(nb, t, a.shape[-1])
        oa = _band_attention(seq(qa), seq(ka16), seq(va16), bias_p[l], qoff=0)
        ob = _stick_breaking(seq(qb), seq(kb16), seq(vb16), qoff=0)
        u3 = seq(u)
        xp = _merge(u3, u3, oa, ob, seq(gate), seq(xp), *mix_w, tm=512, hist_from_u=True)
        xp = _ffn(xp.reshape(nb * t, d), ln_ffn2[l], f2g[l], f2u[l], f2d[l], final_norm, final=last)
        heads = lambda a, h: a.reshape(nb, h, HEAD_DIM, a.shape[-1]).transpose(0, 3, 1, 2)
        tail = lambda a, h: heads(a[:, :, t - a_len:], h)
        p_state.append((tail(ka32, h_a), tail(va32, h_a),
                        heads(kb32, h_b), heads(vb32, h_b), u3[:, t - (CONV_W - 1):]))

        xs = _ffn(xs, ln_ffn1[l], f1g[l], f1u[l], f1d[l], final_norm, final=False)
        (qa, ka32, va32, ka16, va16, qb, kb32, vb32, kb16, vb16, u, gate) = _inproj(
            xs, ln_mix[l], w_in16[l], b_gate[l], width=wa_width)
        seq = lambda a: a.reshape(ns, ts, a.shape[-1])
        with_new = lambda padded, at, new: padded[l].at[:, at:at + ts].set(seq(new))
        oa = _band_attention(_pad_rows(seq(qa), BAND_BLOCK), with_new(seq_a_k, a_len, ka16),
                             with_new(seq_a_v, a_len, va16), bias_s[l], qoff=band_qoff_s)[:, :ts]
        ob = _stick_breaking(_pad_rows(seq(qb), SB_BLOCK), with_new(seq_b_k, past, kb16),
                             with_new(seq_b_v, past, vb16), qoff=sb_qoff_s)[:, :ts]
        u3 = seq(u)
        xs = _merge(u3, hist[l], oa, ob, seq(gate), seq(xs), *mix_w, tm=ts, hist_from_u=False)
        xs = _ffn(xs.reshape(ns * ts, d), ln_ffn2[l], f2g[l], f2u[l], f2d[l], final_norm, final=last)
        heads = lambda a, h: a.reshape(ns, ts, h, HEAD_DIM)
        up = jnp.concatenate([state_conv[l], u3], axis=1)
        s_state.append((heads(ka32, h_a), heads(va32, h_a), heads(kb32, h_b), heads(vb32, h_b),
                        up[:, ts:]))

    stack = lambda states, i: jnp.stack([s[i] for s in states], axis=0)
    return (xp.reshape(nb, t, d), xs.reshape(ns, ts, d),
            stack(p_state, 0), stack(p_state, 1), stack(p_state, 2), stack(p_state, 3), stack(p_state, 4),
            stack(s_state, 0), stack(s_state, 1), stack(s_state, 2), stack(s_state, 3), stack(s_state, 4))
```

```python
import functools

import numpy as np
import jax
import jax.numpy as jnp
from jax import lax
from jax.experimental import pallas as pl
from jax.experimental.pallas import tpu as pltpu

F32 = jnp.float32
BF16 = jnp.bfloat16

HEAD_DIM = 64
CHUNK = 64
A_PAST_CHUNKS = 8
REL_MIN = -(CHUNK - 1)
REL_MAX = 128
CONV_W = 31
RMS_EPS = 1e-6
LN_EPS = 1e-5
NEG_INF = -1e30
LOG2_E = 1.4426950408889634
Q_SCALE = HEAD_DIM ** -0.5 * LOG2_E

LANES = 128
SUBLANES = 8
MXU_EDGE = 256
BAND_BLOCK = 4 * CHUNK
BAND_NBLK = A_PAST_CHUNKS * CHUNK // BAND_BLOCK + 1
SB_BLOCK = 256
SB_HEADS = 4
SB_DEAD = -175.0
CONV_HIST = 32
VMEM_LIMIT = 56 * 1024 * 1024


def _params(n_axes, vmem=VMEM_LIMIT):
    return pltpu.CompilerParams(dimension_semantics=("arbitrary",) * n_axes, vmem_limit_bytes=vmem)


def _resident(shape):
    nd = len(shape)
    return pl.BlockSpec(shape, lambda *_: (0,) * nd, pipeline_mode=pl.Buffered(1))


def _row_tile(m, want):
    tm = want
    while tm > 16 and m % tm:
        tm //= 2
    assert m % tm == 0
    return tm


def _rms(x, g):
    return x * lax.rsqrt(jnp.mean(x * x, axis=-1, keepdims=True) + RMS_EPS) * g


def _sigmoid(x):
    return 1.0 / (1.0 + jnp.exp(-x))


def _dot(a, b):
    return jnp.dot(a, b, preferred_element_type=F32)


def _dot_nt(a, b):
    return lax.dot_general(a, b, (((1,), (1,)), ((), ())), preferred_element_type=F32)


CAST_BLOCK_BYTES = 4 * 1024 * 1024


def _cast_body(x_ref, o_ref):
    o_ref[...] = x_ref[...].astype(BF16)


def _to_bf16(w):
    cols = w.shape[-1]
    w2 = w.reshape(-1, cols)
    rows = w2.shape[0]
    want = 1024
    while want * cols * 4 > CAST_BLOCK_BYTES:
        want //= 2
    tr = _row_tile(rows, want)
    spec = pl.BlockSpec((tr, cols), lambda i: (i, 0))
    out = pl.pallas_call(
        _cast_body,
        out_shape=jax.ShapeDtypeStruct((rows, cols), BF16),
        grid=(rows // tr,),
        in_specs=[spec],
        out_specs=spec,
        compiler_params=_params(1),
        name="cast_bf16",
    )(w2)
    return out.reshape(w.shape)


def _ffn_body(x_ref, g_ref, wg_ref, wu_ref, wd_ref, fg_ref, o_ref, *, cuts, final):
    x = x_ref[...]
    h = _rms(x, g_ref[...]).astype(BF16)
    acc = None
    for lo, hi in zip(cuts[:-1], cuts[1:]):
        a = _dot(h, wg_ref[:, lo:hi])
        b = _dot(h, wu_ref[:, lo:hi])
        act = (a * _sigmoid(a) * b).astype(BF16)
        d = _dot(act, wd_ref[lo:hi, :])
        acc = d if acc is None else acc + d
    y = x + 0.5 * acc
    if final:
        y = _rms(y, fg_ref[...])
    o_ref[...] = y


def _ffn(x, g, wg, wu, wd, fg, *, final):
    m, d = x.shape
    d_ff = wg.shape[1]
    tm = _row_tile(m, 512)
    assert d_ff % MXU_EDGE == 0
    cuts = (0, (d_ff // MXU_EDGE + 1) // 2 * MXU_EDGE, d_ff)
    row = pl.BlockSpec((tm, d), lambda i: (i, 0))
    return pl.pallas_call(
        functools.partial(_ffn_body, cuts=cuts, final=final),
        out_shape=jax.ShapeDtypeStruct((m, d), F32),
        grid=(m // tm,),
        in_specs=[row, _resident((1, d)), _resident((d, d_ff)), _resident((d, d_ff)),
                  _resident((d_ff, d)), _resident((1, d))],
        out_specs=row,
        compiler_params=_params(1),
        name="ffn_half",
    )(x, g.reshape(1, d), wg, wu, wd, fg.reshape(1, d))


def _inproj_body(x_ref, g_ref, w_ref, bg_ref, qa_ref, ka32_ref, va32_ref, ka16_ref, va16_ref,
                 qb_ref, kb32_ref, vb32_ref, kb16_ref, vb16_ref, u_ref, gate_ref, *, width, d_model,
                 time_minor):
    h = _rms(x_ref[...], g_ref[...]).astype(BF16)

    def mm(col, n):
        return _dot(h, w_ref[:, col:col + n])

    def put_state(ref, val):
        if time_minor:
            ref[0] = val.T
        else:
            ref[...] = val

    scale = Q_SCALE
    col = 0
    for q_ref, k32_ref, v32_ref, k16_ref, v16_ref in (
            (qa_ref, ka32_ref, va32_ref, ka16_ref, va16_ref),
            (qb_ref, kb32_ref, vb32_ref, kb16_ref, vb16_ref)):
        q_ref[...] = (mm(col, width) * scale).astype(BF16)
        k = mm(col + width, width)
        put_state(k32_ref, k)
        k16_ref[...] = k.astype(BF16)
        v = mm(col + 2 * width, width)
        put_state(v32_ref, v)
        v16_ref[...] = v.astype(BF16)
        col += 3 * width
    lin = mm(col, width)
    gt = mm(col + width, width)
    u_ref[...] = lin * _sigmoid(gt)
    col += 2 * width
    for j in range(3):
        graw = mm(col + j * d_model, d_model) + bg_ref[:, j * d_model:(j + 1) * d_model]
        gate_ref[:, j * d_model:(j + 1) * d_model] = _sigmoid(graw).astype(BF16)


def _inproj(x, g, w_in, b_gate, *, width, seq_len=None):
    m, d = x.shape
    n_cols = w_in.shape[1]
    tm = _row_tile(m, 256)
    row = lambda n: pl.BlockSpec((tm, n), lambda i: (i, 0))
    sds = lambda n, dt: jax.ShapeDtypeStruct((m, n), dt)
    if seq_len is None:
        state, state_spec = sds(width, F32), row(width)
    else:
        per_seq = seq_len // tm
        state = jax.ShapeDtypeStruct((m // seq_len, width, seq_len), F32)
        state_spec = pl.BlockSpec((1, width, tm), lambda i: (i // per_seq, 0, i % per_seq))
    mixer = [sds(width, BF16), state, state, sds(width, BF16), sds(width, BF16)]
    mixer_specs = [row(width), state_spec, state_spec, row(width), row(width)]
    return pl.pallas_call(
        functools.partial(_inproj_body, width=width, d_model=d, time_minor=seq_len is not None),
        out_shape=mixer + mixer + [sds(width, F32), sds(3 * d, BF16)],
        grid=(m // tm,),
        in_specs=[row(d), _resident((1, d)), _resident((d, n_cols)), _resident((1, 3 * d))],
        out_specs=mixer_specs + mixer_specs + [row(width), row(3 * d)],
        compiler_params=_params(1),
        name="in_projection",
    )(x, g.reshape(1, d), w_in, b_gate.reshape(1, 3 * d))


def _band_body(q_ref, k_ref, v_ref, bias_ref, o_ref, *, qoff, n_heads):
    blk = pl.program_id(1) + qoff
    low = lax.broadcasted_iota(jnp.int32, (BAND_BLOCK, LANES), 1) < HEAD_DIM
    first = [blk + j - (BAND_NBLK - 1) for j in range(BAND_NBLK)]
    starts = [pl.multiple_of(jnp.maximum(f, 0) * BAND_BLOCK, BAND_BLOCK) for f in first]
    for hp in range(n_heads // 2):
        lanes = slice(hp * LANES, (hp + 1) * LANES)
        q = q_ref[0, :, lanes].astype(F32)
        q2 = jnp.concatenate([jnp.where(low, q, 0.0), jnp.where(low, 0.0, q)], axis=0).astype(BF16)
        s = []
        for j in range(BAND_NBLK):
            sj = _dot_nt(q2, k_ref[0, pl.ds(starts[j], BAND_BLOCK), lanes]) + bias_ref[hp, j]
            if j < BAND_NBLK - 1:
                sj = jnp.where(first[j] >= 0, sj, NEG_INF)
            s.append(sj)
        mx = functools.reduce(jnp.maximum, [jnp.max(sj, axis=-1, keepdims=True) for sj in s])
        p = [jnp.exp2(sj - mx) for sj in s]
        den = functools.reduce(jnp.add, [jnp.sum(pj, axis=-1, keepdims=True) for pj in p])
        o = functools.reduce(jnp.add, [
            _dot(pj.astype(BF16), v_ref[0, pl.ds(starts[j], BAND_BLOCK), lanes])
            for j, pj in enumerate(p)])
        o = o * (1.0 / den)
        o_ref[0, :, lanes] = jnp.where(low, o[:BAND_BLOCK], o[BAND_BLOCK:]).astype(BF16)


def _band_attention(q, k, v, bias, *, qoff):
    n, tq, w = q.shape
    tk = k.shape[1]
    n_heads = w // HEAD_DIM
    qspec = pl.BlockSpec((1, BAND_BLOCK, w), lambda b, i: (b, i, 0))
    kspec = pl.BlockSpec((1, tk, w), lambda b, i: (b, 0, 0), pipeline_mode=pl.Buffered(1))
    return pl.pallas_call(
        functools.partial(_band_body, qoff=qoff, n_heads=n_heads),
        out_shape=jax.ShapeDtypeStruct((n, tq, w), BF16),
        grid=(n, tq // BAND_BLOCK),
        in_specs=[qspec, kspec, kspec, _resident(bias.shape)],
        out_specs=qspec,
        compiler_params=_params(2),
        name="band_attention",
    )(q, k, v, bias)


def _band_bias(rel_bias, kv_len):
    nb = BAND_BLOCK
    r = np.arange(nb)[:, None]
    c = np.arange(nb)[None, :]
    ok = np.zeros((BAND_NBLK, nb, nb), bool)
    real = np.zeros((BAND_NBLK, nb, nb), bool)
    m = np.arange(2 * nb)
    diff = np.where(m < nb, -m, 2 * nb - m)
    line_idx = np.zeros((BAND_NBLK, 2 * nb), np.int32)
    for j in range(BAND_NBLK):
        line_idx[j] = np.clip(diff + nb * (BAND_NBLK - 1 - j), REL_MIN, REL_MAX) - REL_MIN
        dchunk = (A_PAST_CHUNKS + r // CHUNK) - (c // CHUNK + (nb // CHUNK) * j)
        ok[j] = (dchunk >= 0) & (dchunk <= A_PAST_CHUNKS)
        real[j] = np.broadcast_to((nb * j + c) < kv_len, (nb, nb))
    depth, heads = rel_bias.shape[:2]
    line = (rel_bias.astype(F32) * LOG2_E)[:, :, line_idx]
    line = line.reshape(depth, heads // 2, 2, BAND_NBLK, 2 * nb).transpose(0, 1, 3, 2, 4)
    table = jnp.tile(line, (1, 1, 1, 1, nb))[..., :nb * (2 * nb - 1)]
    table = table.reshape(depth, heads // 2, BAND_NBLK, 2, nb, 2 * nb - 1)[..., :nb]
    table = jnp.where(ok[None, None, :, None], table, NEG_INF)
    padded = jnp.where(real[None, None, :, None], table, NEG_INF)
    shape = (depth, heads // 2, BAND_NBLK, 2 * nb, nb)
    return table.reshape(shape), padded.reshape(shape)


def _sb_body(q_ref, k_ref, v_ref, tri_ref, o_ref, *, qoff):
    blk = pl.program_id(2) + qoff
    width = SB_HEADS * HEAD_DIM
    rows = SB_HEADS * SB_BLOCK
    lane = lax.broadcasted_iota(jnp.int32, (SB_BLOCK, width), 1)
    q = q_ref[0].astype(F32)
    qs = jnp.concatenate(
        [jnp.where((lane >= h * HEAD_DIM) & (lane < (h + 1) * HEAD_DIM), q, 0.0)
         for h in range(SB_HEADS)], axis=0).astype(BF16)
    tri = tri_ref[...]
    row = lax.broadcasted_iota(jnp.int32, (rows, SB_BLOCK), 0) & (SB_BLOCK - 1)
    col = lax.broadcasted_iota(jnp.int32, (rows, SB_BLOCK), 1)
    causal = col < row
    low = lax.broadcasted_iota(jnp.int32, (SB_BLOCK, LANES), 1) < HEAD_DIM

    def sweep(kb, acc, later, diagonal, valid=None):
        start = pl.multiple_of(kb * SB_BLOCK, SB_BLOCK)
        k = k_ref[0, pl.ds(start, SB_BLOCK), :]
        v = v_ref[0, pl.ds(start, SB_BLOCK), :]
        z = _dot_nt(qs, k)
        drop = jnp.maximum(z, 0.0) + jnp.log2(1.0 + jnp.exp2(-jnp.abs(z)))
        if diagonal:
            drop = jnp.where(causal, drop, 0.0)
        suffix = _dot(drop.astype(BF16), tri)
        a = jnp.exp2(z + suffix + later)
        if diagonal:
            a = jnp.where(causal, a, 0.0)
        pv = _dot(a.astype(BF16), v)
        pieces = [pv[h * SB_BLOCK:(h + 1) * SB_BLOCK, (h // 2) * LANES:(h // 2 + 1) * LANES]
                  for h in range(SB_HEADS)]
        picked = jnp.concatenate([jnp.where(low, pieces[0], pieces[1]),
                                  jnp.where(low, pieces[2], pieces[3])], axis=1)
        total = suffix[:, 0:1]
        if valid is not None:
            picked = jnp.where(valid, picked, 0.0)
            total = jnp.where(valid, total, 0.0)
        return acc + picked, later + total

    acc = jnp.zeros((SB_BLOCK, width), F32)
    later = jnp.zeros((rows, 1), F32)
    acc, later = sweep(blk, acc, later, True)
    acc, later = sweep(jnp.maximum(blk - 1, 0), acc, later, False, valid=blk > 0)

    def body(carry):
        _, kb, acc, later = carry
        acc, later = sweep(kb, acc, later, False)
        alive = jnp.max(later) > SB_DEAD
        more = jnp.logical_and(kb > 0, alive).astype(jnp.int32)
        return more, kb - 1, acc, later

    start = jnp.logical_and(blk > 1, jnp.max(later) > SB_DEAD).astype(jnp.int32)
    _, _, acc, later = lax.while_loop(lambda c: c[0] > 0, body, (start, blk - 2, acc, later))
    o_ref[0] = acc.astype(BF16)


def _stick_breaking(q, k, v, *, qoff):
    n, tq, w = q.shape
    tk = k.shape[1]
    gw = SB_HEADS * HEAD_DIM
    tri = jnp.asarray(-np.tril(np.ones((SB_BLOCK, SB_BLOCK), np.float32)), BF16)
    qspec = pl.BlockSpec((1, SB_BLOCK, gw), lambda b, g, i: (b, i, g))
    kspec = pl.BlockSpec((1, tk, gw), lambda b, g, i: (b, 0, g))
    return pl.pallas_call(
        functools.partial(_sb_body, qoff=qoff),
        out_shape=jax.ShapeDtypeStruct((n, tq, w), BF16),
        grid=(n, w // gw, tq // SB_BLOCK),
        in_specs=[qspec, kspec, kspec, _resident((SB_BLOCK, SB_BLOCK))],
        out_specs=qspec,
        compiler_params=_params(3),
        name="stick_breaking",
    )(q, k, v, tri)


def _merge_body(u_ref, hist_ref, oa_ref, ob_ref, gate_ref, x_ref, cw_ref, cb_ref, lg_ref, lb_ref,
                wc_ref, wa_ref, wb_ref, wo_ref, o_ref, up_ref, rot_ref, *, zero_first_hist):
    tm = u_ref.shape[1]
    d = x_ref.shape[2]
    hist = hist_ref[0]
    if zero_first_hist:
        hist = jnp.where(pl.program_id(1) > 0, hist, 0.0)
    up_ref[0:CONV_HIST, :] = hist
    up_ref[CONV_HIST:CONV_HIST + tm, :] = u_ref[0]
    span = rot_ref.shape[1]
    for b in range(1, SUBLANES):
        rot_ref[b - 1] = up_ref[b:b + span, :]
    first = CONV_HIST - (CONV_W - 1)

    def rows_out(r0, n):
        y = None
        for w in range(CONV_W):
            a, b = divmod(first + w, SUBLANES)
            lo = SUBLANES * a + r0
            window = up_ref[lo:lo + n, :] if b == 0 else rot_ref[b - 1, lo:lo + n, :]
            term = window * cw_ref[w:w + 1, :]
            y = term if y is None else y + term
        y = y + cb_ref[...]
        mu = jnp.mean(y, axis=-1, keepdims=True)
        yc = y - mu
        var = jnp.mean(yc * yc, axis=-1, keepdims=True)
        yn = yc * lax.rsqrt(var + LN_EPS) * lg_ref[...] + lb_ref[...]
        oc = _dot((yn * _sigmoid(yn)).astype(BF16), wc_ref[...])
        pa = _dot(oa_ref[0, r0:r0 + n, :], wa_ref[...])
        pb = _dot(ob_ref[0, r0:r0 + n, :], wb_ref[...])
        merged = (gate_ref[0, r0:r0 + n, 0:d].astype(F32) * pa
                  + gate_ref[0, r0:r0 + n, d:2 * d].astype(F32) * pb
                  + gate_ref[0, r0:r0 + n, 2 * d:3 * d].astype(F32) * oc)
        o_ref[0, r0:r0 + n, :] = x_ref[0, r0:r0 + n, :] + _dot(merged.astype(BF16), wo_ref[...])

    halves = 2 if tm % (2 * MXU_EDGE) == 0 else 1
    for part in range(halves):
        rows_out(part * (tm // halves), tm // halves)


def _merge(u, hist, oa, ob, gate, x, cw, cb, lg, lb, wc, wa, wb, wo, *, tm, hist_from_u):
    n, t, c = u.shape
    d = x.shape[2]
    tile = lambda ch: pl.BlockSpec((1, tm, ch), lambda b, i: (b, i, 0))
    if hist_from_u:
        per = tm // CONV_HIST
        hspec = pl.BlockSpec((1, CONV_HIST, c), lambda b, i: (b, jnp.maximum(i * per - 1, 0), 0))
    else:
        hspec = pl.BlockSpec((1, CONV_HIST, c), lambda b, i: (b, 0, 0))
    vec = lambda a: a.reshape(1, -1)
    return pl.pallas_call(
        functools.partial(_merge_body, zero_first_hist=hist_from_u),
        out_shape=jax.ShapeDtypeStruct((n, t, d), F32),
        grid=(n, t // tm),
        in_specs=[tile(c), hspec, tile(oa.shape[2]), tile(ob.shape[2]), tile(3 * d), tile(d),
                  _resident(cw.shape), _resident((1, c)), _resident((1, c)), _resident((1, c)),
                  _resident(wc.shape), _resident(wa.shape), _resident(wb.shape), _resident(wo.shape)],
        out_specs=tile(d),
        scratch_shapes=[pltpu.VMEM((CONV_HIST + tm, c), F32),
                        pltpu.VMEM((SUBLANES - 1, CONV_HIST + tm - SUBLANES, c), F32)],
        compiler_params=_params(2),
        name="conv_merge",
    )(u, hist, oa, ob, gate, x, cw, vec(cb), vec(lg), vec(lb), wc, wa, wb, wo)


def _pad_rows(a, rows):
    return jnp.pad(a, ((0, 0), (0, rows - a.shape[1]), (0, 0)))


def kernel(x_prompt, x_sample, cache_a_k, cache_a_v, cache_b_k, cache_b_v, state_conv, w_in, b_gate,
           rel_bias, w_a_out, w_b_out, conv_w, conv_b, conv_ln_g, conv_ln_b, w_c_out, w_o, ln_ffn1,
           ffn1_w_gate, ffn1_w_up, ffn1_w_down, ln_mix, ln_ffn2, ffn2_w_gate, ffn2_w_up, ffn2_w_down,
           final_norm):
    depth = w_in.shape[0]
    nb, t, d = x_prompt.shape
    ns, ts, _ = x_sample.shape
    a_len = cache_a_k.shape[2]
    past = cache_b_k.shape[2]
    h_a, h_b = cache_a_k.shape[3], cache_b_k.shape[3]
    wa_width, wb_width = h_a * HEAD_DIM, h_b * HEAD_DIM
    c_conv = state_conv.shape[3]
    assert wa_width == wb_width == c_conv
    assert t % 512 == 0 and a_len % BAND_BLOCK == 0 and past % SB_BLOCK == 0
    assert a_len == A_PAST_CHUNKS * CHUNK and past % CHUNK == 0 and ts <= CHUNK and ts % 16 == 0
    assert t >= a_len

    bf = lambda a: a.astype(BF16)
    w_in16, wa16, wb16, wc16, wo16 = map(_to_bf16, (w_in, w_a_out, w_b_out, w_c_out, w_o))
    f1g, f1u, f1d = map(_to_bf16, (ffn1_w_gate, ffn1_w_up, ffn1_w_down))
    f2g, f2u, f2d = map(_to_bf16, (ffn2_w_gate, ffn2_w_up, ffn2_w_down))

    bias_p, bias_s = _band_bias(rel_bias, kv_len=a_len + ts)
    band_qoff_s = a_len // BAND_BLOCK
    sb_qoff_s = past // SB_BLOCK

    def padded_cache(cache, extra):
        flat = bf(cache.reshape(*cache.shape[:3], -1))
        return jnp.pad(flat, ((0, 0), (0, 0), (0, extra), (0, 0)))

    seq_a_k, seq_a_v = padded_cache(cache_a_k, BAND_BLOCK), padded_cache(cache_a_v, BAND_BLOCK)
    seq_b_k, seq_b_v = padded_cache(cache_b_k, SB_BLOCK), padded_cache(cache_b_v, SB_BLOCK)
    hist = jnp.pad(state_conv, ((0, 0), (0, 0), (CONV_HIST - (CONV_W - 1), 0), (0, 0)))

    xp = x_prompt.reshape(nb * t, d)
    xs = x_sample.reshape(ns * ts, d)
    p_state, s_state = [], []
    for l in range(depth):
        last = l == depth - 1
        mix_w = (conv_w[l], conv_b[l], conv_ln_g[l], conv_ln_b[l], wc16[l], wa16[l], wb16[l], wo16[l])

        xp = _ffn(xp, ln_ffn1[l], f1g[l], f1u[l], f1d[l], final_norm, final=False)
        (qa, ka32, va32, ka16, va16, qb, kb32, vb32, kb16, vb16, u, gate) = _inproj(
            xp, ln_mix[l], w_in16[l], b_gate[l], width=wa_width, seq_len=t)
        seq = lambda a: a.reshape(nb, t, a.shape[-1])
        oa = _band_attention(seq(qa), seq(ka16), seq(va16), bias_p[l], qoff=0)
        ob = _stick_breaking(seq(qb), seq(kb16), seq(vb16), qoff=0)
        u3 = seq(u)
        xp = _merge(u3, u3, oa, ob, seq(gate), seq(xp), *mix_w, tm=512, hist_from_u=True)
        xp = _ffn(xp.reshape(nb * t, d), ln_ffn2[l], f2g[l], f2u[l], f2d[l], final_norm, final=last)
        heads = lambda a, h: a.reshape(nb, h, HEAD_DIM, a.shape[-1]).transpose(0, 3, 1, 2)
        tail = lambda a, h: heads(a[:, :, t - a_len:], h)
        p_state.append((tail(ka32, h_a), tail(va32, h_a),
                        heads(kb32, h_b), heads(vb32, h_b), u3[:, t - (CONV_W - 1):]))

        xs = _ffn(xs, ln_ffn1[l], f1g[l], f1u[l], f1d[l], final_norm, final=False)
        (qa, ka32, va32, ka16, va16, qb, kb32, vb32, kb16, vb16, u, gate) = _inproj(
            xs, ln_mix[l], w_in16[l], b_gate[l], width=wa_width)
        seq = lambda a: a.reshape(ns, ts, a.shape[-1])
        with_new = lambda padded, at, new: padded[l].at[:, at:at + ts].set(seq(new))
        oa = _band_attention(_pad_rows(seq(qa), BAND_BLOCK), with_new(seq_a_k, a_len, ka16),
                             with_new(seq_a_v, a_len, va16), bias_s[l], qoff=band_qoff_s)[:, :ts]
        ob = _stick_breaking(_pad_rows(seq(qb), SB_BLOCK), with_new(seq_b_k, past, kb16),
                             with_new(seq_b_v, past, vb16), qoff=sb_qoff_s)[:, :ts]
        u3 = seq(u)
        xs = _merge(u3, hist[l], oa, ob, seq(gate), seq(xs), *mix_w, tm=ts, hist_from_u=False)
        xs = _ffn(xs.reshape(ns * ts, d), ln_ffn2[l], f2g[l], f2u[l], f2d[l], final_norm, final=last)
        heads = lambda a, h: a.reshape(ns, ts, h, HEAD_DIM)
        up = jnp.concatenate([state_conv[l], u3], axis=1)
        s_state.append((heads(ka32, h_a), heads(va32, h_a), heads(kb32, h_b), heads(vb32, h_b),
                        up[:, ts:]))

    stack = lambda states, i: jnp.stack([s[i] for s in states], axis=0)
    return (xp.reshape(nb, t, d), xs.reshape(ns, ts, d),
            stack(p_state, 0), stack(p_state, 1), stack(p_state, 2), stack(p_state, 3), stack(p_state, 4),
            stack(s_state, 0), stack(s_state, 1), stack(s_state, 2), stack(s_state, 3), stack(s_state, 4))
```

```python
import functools

import numpy as np
import jax
import jax.numpy as jnp
from jax import lax
from jax.experimental import pallas as pl
from jax.experimental.pallas import tpu as pltpu

F32 = jnp.float32
BF16 = jnp.bfloat16

HEAD_DIM = 64
CHUNK = 64
A_PAST_CHUNKS = 8
REL_MIN = -(CHUNK - 1)
REL_MAX = 128
CONV_W = 31
RMS_EPS = 1e-6
LN_EPS = 1e-5
NEG_INF = -1e30
LOG2_E = 1.4426950408889634
Q_SCALE = HEAD_DIM ** -0.5 * LOG2_E

LANES = 128
SUBLANES = 8
MXU_EDGE = 256
BAND_BLOCK = 4 * CHUNK
BAND_NBLK = A_PAST_CHUNKS * CHUNK // BAND_BLOCK + 1
SB_BLOCK = 256
SB_HEADS = 4
SB_DEAD = -175.0
CONV_HIST = 32
VMEM_LIMIT = 56 * 1024 * 1024


def _params(n_axes, vmem=VMEM_LIMIT):
    return pltpu.CompilerParams(dimension_semantics=("arbitrary",) * n_axes, vmem_limit_bytes=vmem)


def _resident(shape):
    nd = len(shape)
    return pl.BlockSpec(shape, lambda *_: (0,) * nd, pipeline_mode=pl.Buffered(1))


def _row_tile(m, want):
    tm = want
    while tm > 16 and m % tm:
        tm //= 2
    assert m % tm == 0
    return tm


def _rms(x, g):
    return x * lax.rsqrt(jnp.mean(x * x, axis=-1, keepdims=True) + RMS_EPS) * g


def _sigmoid(x):
    return 1.0 / (1.0 + jnp.exp(-x))


def _dot(a, b):
    return jnp.dot(a, b, preferred_element_type=F32)


def _dot_nt(a, b):
    return lax.dot_general(a, b, (((1,), (1,)), ((), ())), preferred_element_type=F32)


CAST_BLOCK_BYTES = 4 * 1024 * 1024


def _cast_body(x_ref, o_ref):
    o_ref[...] = x_ref[...].astype(BF16)


def _to_bf16(w):
    cols = w.shape[-1]
    w2 = w.reshape(-1, cols)
    rows = w2.shape[0]
    want = 1024
    while want * cols * 4 > CAST_BLOCK_BYTES:
        want //= 2
    tr = _row_tile(rows, want)
    spec = pl.BlockSpec((tr, cols), lambda i: (i, 0))
    out = pl.pallas_call(
        _cast_body,
        out_shape=jax.ShapeDtypeStruct((rows, cols), BF16),
        grid=(rows // tr,),
        in_specs=[spec],
        out_specs=spec,
        compiler_params=_params(1),
        name="cast_bf16",
    )(w2)
    return out.reshape(w.shape)


def _ffn_body(x_ref, g_ref, wg_ref, wu_ref, wd_ref, fg_ref, o_ref, *, cuts, final):
    x = x_ref[...]
    h = _rms(x, g_ref[...]).astype(BF16)
    acc = None
    for lo, hi in zip(cuts[:-1], cuts[1:]):
        a = _dot(h, wg_ref[:, lo:hi])
        b = _dot(h, wu_ref[:, lo:hi])
        act = (a * _sigmoid(a) * b).astype(BF16)
        d = _dot(act, wd_ref[lo:hi, :])
        acc = d if acc is None else acc + d
    y = x + 0.5 * acc
    if final:
        y = _rms(y, fg_ref[...])
    o_ref[...] = y


def _ffn(x, g, wg, wu, wd, fg, *, final):
    m, d = x.shape
    d_ff = wg.shape[1]
    tm = _row_tile(m, 512)
    assert d_ff % MXU_EDGE == 0
    cuts = (0, (d_ff // MXU_EDGE + 1) // 2 * MXU_EDGE, d_ff)
    row = pl.BlockSpec((tm, d), lambda i: (i, 0))
    return pl.pallas_call(
        functools.partial(_ffn_body, cuts=cuts, final=final),
        out_shape=jax.ShapeDtypeStruct((m, d), F32),
        grid=(m // tm,),
        in_specs=[row, _resident((1, d)), _resident((d, d_ff)), _resident((d, d_ff)),
                  _resident((d_ff, d)), _resident((1, d))],
        out_specs=row,
        compiler_params=_params(1),
        name="ffn_half",
    )(x, g.reshape(1, d), wg, wu, wd, fg.reshape(1, d))


def _inproj_body(x_ref, g_ref, bg_ref, *rest, width, d_model, time_minor):
    w_refs = rest[:-12]
    (qa_ref, ka32_ref, va32_ref, ka16_ref, va16_ref,
     qb_ref, kb32_ref, vb32_ref, kb16_ref, vb16_ref, u_ref, gate_ref) = rest[-12:]
    h = _rms(x_ref[...], g_ref[...]).astype(BF16)

    def mm(col, n):
        slab, at = divmod(col, d_model)
        assert at + n <= d_model
        return _dot(h, w_refs[slab][:, at:at + n])

    def put_state(ref, val):
        if time_minor:
            ref[0] = val.T
        else:
            ref[...] = val

    scale = Q_SCALE
    col = 0
    for q_ref, k32_ref, v32_ref, k16_ref, v16_ref in (
            (qa_ref, ka32_ref, va32_ref, ka16_ref, va16_ref),
            (qb_ref, kb32_ref, vb32_ref, kb16_ref, vb16_ref)):
        q_ref[...] = (mm(col, width) * scale).astype(BF16)
        k = mm(col + width, width)
        put_state(k32_ref, k)
        k16_ref[...] = k.astype(BF16)
        v = mm(col + 2 * width, width)
        put_state(v32_ref, v)
        v16_ref[...] = v.astype(BF16)
        col += 3 * width
    lin = mm(col, width)
    gt = mm(col + width, width)
    u_ref[...] = lin * _sigmoid(gt)
    col += 2 * width
    for j in range(3):
        graw = mm(col + j * d_model, d_model) + bg_ref[:, j * d_model:(j + 1) * d_model]
        gate_ref[:, j * d_model:(j + 1) * d_model] = _sigmoid(graw).astype(BF16)


def _inproj(x, g, w_in, b_gate, *, width, seq_len=None):
    m, d = x.shape
    n_cols = w_in.shape[1]
    tm = _row_tile(m, 256)
    row = lambda n: pl.BlockSpec((tm, n), lambda i: (i, 0))
    sds = lambda n, dt: jax.ShapeDtypeStruct((m, n), dt)
    if seq_len is None:
        state, state_spec = sds(width, F32), row(width)
    else:
        per_seq = seq_len // tm
        state = jax.ShapeDtypeStruct((m // seq_len, width, seq_len), F32)
        state_spec = pl.BlockSpec((1, width, tm), lambda i: (i // per_seq, 0, i % per_seq))
    mixer = [sds(width, BF16), state, state, sds(width, BF16), sds(width, BF16)]
    mixer_specs = [row(width), state_spec, state_spec, row(width), row(width)]
    n_slabs = n_cols // d
    assert n_cols % d == 0 and d % width == 0
    slabs = [pl.BlockSpec((d, d), functools.partial(lambda i, s: (0, s), s=s),
                          pipeline_mode=pl.Buffered(1)) for s in range(n_slabs)]
    return pl.pallas_call(
        functools.partial(_inproj_body, width=width, d_model=d, time_minor=seq_len is not None),
        out_shape=mixer + mixer + [sds(width, F32), sds(3 * d, BF16)],
        grid=(m // tm,),
        in_specs=[row(d), _resident((1, d)), _resident((1, 3 * d))] + slabs,
        out_specs=mixer_specs + mixer_specs + [row(width), row(3 * d)],
        compiler_params=_params(1),
        name="in_projection",
    )(x, g.reshape(1, d), b_gate.reshape(1, 3 * d), *([w_in] * n_slabs))


def _band_body(q_ref, *rest, qoff, n_pairs):
    k_refs, v_refs = rest[:n_pairs], rest[n_pairs:2 * n_pairs]
    bias_ref, o_ref = rest[2 * n_pairs:]
    rows = q_ref.shape[1]
    blk = pl.program_id(1) + qoff
    low = lax.broadcasted_iota(jnp.int32, (rows, LANES), 1) < HEAD_DIM
    first = [blk + j - (BAND_NBLK - 1) for j in range(BAND_NBLK)]
    starts = [pl.multiple_of(jnp.maximum(f, 0) * BAND_BLOCK, BAND_BLOCK) for f in first]
    for hp in range(n_pairs):
        lanes = slice(hp * LANES, (hp + 1) * LANES)
        q = q_ref[0, :, lanes].astype(F32)
        q2 = jnp.concatenate([jnp.where(low, q, 0.0), jnp.where(low, 0.0, q)], axis=0).astype(BF16)
        s = []
        for j in range(BAND_NBLK):
            bias = bias_ref[hp, j]
            if rows < BAND_BLOCK:
                bias = jnp.concatenate([bias[0:rows], bias[BAND_BLOCK:BAND_BLOCK + rows]], axis=0)
            sj = _dot_nt(q2, k_refs[hp][0, pl.ds(starts[j], BAND_BLOCK), :]) + bias
            if j < BAND_NBLK - 1:
                sj = jnp.where(first[j] >= 0, sj, NEG_INF)
            s.append(sj)
        mx = functools.reduce(jnp.maximum, [jnp.max(sj, axis=-1, keepdims=True) for sj in s])
        p = [jnp.exp2(sj - mx) for sj in s]
        den = functools.reduce(jnp.add, [jnp.sum(pj, axis=-1, keepdims=True) for pj in p])
        o = functools.reduce(jnp.add, [
            _dot(pj.astype(BF16), v_refs[hp][0, pl.ds(starts[j], BAND_BLOCK), :])
            for j, pj in enumerate(p)])
        o = o * (1.0 / den)
        o_ref[0, :, lanes] = jnp.where(low, o[:rows], o[rows:]).astype(BF16)


def _band_attention(q, k, v, bias, *, qoff):
    n, tq, w = q.shape
    tk = k.shape[1]
    n_pairs = w // LANES
    rows = min(tq, BAND_BLOCK)
    qspec = pl.BlockSpec((1, rows, w), lambda b, i: (b, i, 0))
    kspecs = [pl.BlockSpec((1, tk, LANES), functools.partial(lambda b, i, hp: (b, 0, hp), hp=hp),
                           pipeline_mode=pl.Buffered(1)) for hp in range(n_pairs)]
    return pl.pallas_call(
        functools.partial(_band_body, qoff=qoff, n_pairs=n_pairs),
        out_shape=jax.ShapeDtypeStruct((n, tq, w), BF16),
        grid=(n, tq // rows),
        in_specs=[qspec] + kspecs + kspecs + [_resident(bias.shape)],
        out_specs=qspec,
        compiler_params=_params(2),
        name="band_attention",
    )(q, *([k] * n_pairs), *([v] * n_pairs), bias)


def _band_bias(rel_bias, kv_len):
    nb = BAND_BLOCK
    r = np.arange(nb)[:, None]
    c = np.arange(nb)[None, :]
    ok = np.zeros((BAND_NBLK, nb, nb), bool)
    real = np.zeros((BAND_NBLK, nb, nb), bool)
    m = np.arange(2 * nb)
    diff = np.where(m < nb, -m, 2 * nb - m)
    line_idx = np.zeros((BAND_NBLK, 2 * nb), np.int32)
    for j in range(BAND_NBLK):
        line_idx[j] = np.clip(diff + nb * (BAND_NBLK - 1 - j), REL_MIN, REL_MAX) - REL_MIN
        dchunk = (A_PAST_CHUNKS + r // CHUNK) - (c // CHUNK + (nb // CHUNK) * j)
        ok[j] = (dchunk >= 0) & (dchunk <= A_PAST_CHUNKS)
        real[j] = np.broadcast_to((nb * j + c) < kv_len, (nb, nb))
    depth, heads = rel_bias.shape[:2]
    line = (rel_bias.astype(F32) * LOG2_E)[:, :, line_idx]
    line = line.reshape(depth, heads // 2, 2, BAND_NBLK, 2 * nb).transpose(0, 1, 3, 2, 4)
    table = jnp.tile(line, (1, 1, 1, 1, nb))[..., :nb * (2 * nb - 1)]
    table = table.reshape(depth, heads // 2, BAND_NBLK, 2, nb, 2 * nb - 1)[..., :nb]
    table = jnp.where(ok[None, None, :, None], table, NEG_INF)
    padded = jnp.where(real[None, None, :, None], table, NEG_INF)
    shape = (depth, heads // 2, BAND_NBLK, 2 * nb, nb)
    return table.reshape(shape), padded.reshape(shape)


def _sb_body(q_ref, k_ref, v_ref, tri_ref, o_ref, *, qoff):
    blk = pl.program_id(2) + qoff
    width = SB_HEADS * HEAD_DIM
    tq = q_ref.shape[1]
    rows = SB_HEADS * tq
    lane = lax.broadcasted_iota(jnp.int32, (tq, width), 1)
    q = q_ref[0].astype(F32)
    qs = jnp.concatenate(
        [jnp.where((lane >= h * HEAD_DIM) & (lane < (h + 1) * HEAD_DIM), q, 0.0)
         for h in range(SB_HEADS)], axis=0).astype(BF16)
    tri = tri_ref[...]
    row = lax.broadcasted_iota(jnp.int32, (rows, SB_BLOCK), 0) & (tq - 1)
    col = lax.broadcasted_iota(jnp.int32, (rows, SB_BLOCK), 1)
    causal = col < row
    low = lax.broadcasted_iota(jnp.int32, (tq, LANES), 1) < HEAD_DIM

    def sweep(kb, acc, later, diagonal, valid=None):
        start = pl.multiple_of(kb * SB_BLOCK, SB_BLOCK)
        k = k_ref[0, pl.ds(start, SB_BLOCK), :]
        v = v_ref[0, pl.ds(start, SB_BLOCK), :]
        z = _dot_nt(qs, k)
        drop = jnp.maximum(z, 0.0) + jnp.log2(1.0 + jnp.exp2(-jnp.abs(z)))
        if diagonal:
            drop = jnp.where(causal, drop, 0.0)
        suffix = _dot(drop.astype(BF16), tri)
        a = jnp.exp2(z + suffix + later)
        if diagonal:
            a = jnp.where(causal, a, 0.0)
        pv = _dot(a.astype(BF16), v)
        pieces = [pv[h * tq:(h + 1) * tq, (h // 2) * LANES:(h // 2 + 1) * LANES]
                  for h in range(SB_HEADS)]
        picked = jnp.concatenate([jnp.where(low, pieces[0], pieces[1]),
                                  jnp.where(low, pieces[2], pieces[3])], axis=1)
        total = suffix[:, 0:1]
        if valid is not None:
            picked = jnp.where(valid, picked, 0.0)
            total = jnp.where(valid, total, 0.0)
        return acc + picked, later + total

    acc = jnp.zeros((tq, width), F32)
    later = jnp.zeros((rows, 1), F32)
    acc, later = sweep(blk, acc, later, True)
    acc, later = sweep(jnp.maximum(blk - 1, 0), acc, later, False, valid=blk > 0)

    def body(carry):
        _, kb, acc, later = carry
        acc, later = sweep(kb, acc, later, False)
        alive = jnp.max(later) > SB_DEAD
        more = jnp.logical_and(kb > 0, alive).astype(jnp.int32)
        return more, kb - 1, acc, later

    start = jnp.logical_and(blk > 1, jnp.max(later) > SB_DEAD).astype(jnp.int32)
    _, _, acc, later = lax.while_loop(lambda c: c[0] > 0, body, (start, blk - 2, acc, later))
    o_ref[0] = acc.astype(BF16)


def _stick_breaking(q, k, v, *, qoff):
    n, tq, w = q.shape
    tk = k.shape[1]
    gw = SB_HEADS * HEAD_DIM
    rows = min(tq, SB_BLOCK)
    assert rows & (rows - 1) == 0
    tri = jnp.asarray(-np.tril(np.ones((SB_BLOCK, SB_BLOCK), np.float32)), BF16)
    qspec = pl.BlockSpec((1, rows, gw), lambda b, g, i: (b, i, g))
    kspec = pl.BlockSpec((1, tk, gw), lambda b, g, i: (b, 0, g))
    return pl.pallas_call(
        functools.partial(_sb_body, qoff=qoff),
        out_shape=jax.ShapeDtypeStruct((n, tq, w), BF16),
        grid=(n, w // gw, tq // rows),
        in_specs=[qspec, kspec, kspec, _resident((SB_BLOCK, SB_BLOCK))],
        out_specs=qspec,
        compiler_params=_params(3),
        name="stick_breaking",
    )(q, k, v, tri)


def _merge_body(u_ref, hist_ref, oa_ref, ob_ref, gate_ref, x_ref, cw_ref, cb_ref, lg_ref, lb_ref,
                wc_ref, wa_ref, wb_ref, wo_ref, o_ref, up_ref, rot_ref, *, zero_first_hist):
    tm = u_ref.shape[1]
    d = x_ref.shape[2]
    hist = hist_ref[0]
    if zero_first_hist:
        hist = jnp.where(pl.program_id(1) > 0, hist, 0.0)
    up_ref[0:CONV_HIST, :] = hist
    up_ref[CONV_HIST:CONV_HIST + tm, :] = u_ref[0]
    span = rot_ref.shape[1]
    for b in range(1, SUBLANES):
        rot_ref[b - 1] = up_ref[b:b + span, :]
    first = CONV_HIST - (CONV_W - 1)

    def rows_out(r0, n):
        y = None
        for w in range(CONV_W):
            a, b = divmod(first + w, SUBLANES)
            lo = SUBLANES * a + r0
            window = up_ref[lo:lo + n, :] if b == 0 else rot_ref[b - 1, lo:lo + n, :]
            term = window * cw_ref[w:w + 1, :]
            y = term if y is None else y + term
        y = y + cb_ref[...]
        mu = jnp.mean(y, axis=-1, keepdims=True)
        yc = y - mu
        var = jnp.mean(yc * yc, axis=-1, keepdims=True)
        yn = yc * lax.rsqrt(var + LN_EPS) * lg_ref[...] + lb_ref[...]
        oc = _dot((yn * _sigmoid(yn)).astype(BF16), wc_ref[...])
        pa = _dot(oa_ref[0, r0:r0 + n, :], wa_ref[...])
        pb = _dot(ob_ref[0, r0:r0 + n, :], wb_ref[...])
        merged = (gate_ref[0, r0:r0 + n, 0:d].astype(F32) * pa
                  + gate_ref[0, r0:r0 + n, d:2 * d].astype(F32) * pb
                  + gate_ref[0, r0:r0 + n, 2 * d:3 * d].astype(F32) * oc)
        o_ref[0, r0:r0 + n, :] = x_ref[0, r0:r0 + n, :] + _dot(merged.astype(BF16), wo_ref[...])

    halves = 2 if tm % (2 * MXU_EDGE) == 0 else 1
    for part in range(halves):
        rows_out(part * (tm // halves), tm // halves)


def _merge(u, hist, oa, ob, gate, x, cw, cb, lg, lb, wc, wa, wb, wo, *, tm, hist_from_u):
    n, t, c = u.shape
    d = x.shape[2]
    tile = lambda ch: pl.BlockSpec((1, tm, ch), lambda b, i: (b, i, 0))
    if hist_from_u:
        per = tm // CONV_HIST
        hspec = pl.BlockSpec((1, CONV_HIST, c), lambda b, i: (b, jnp.maximum(i * per - 1, 0), 0))
    else:
        hspec = pl.BlockSpec((1, CONV_HIST, c), lambda b, i: (b, 0, 0))
    vec = lambda a: a.reshape(1, -1)
    return pl.pallas_call(
        functools.partial(_merge_body, zero_first_hist=hist_from_u),
        out_shape=jax.ShapeDtypeStruct((n, t, d), F32),
        grid=(n, t // tm),
        in_specs=[tile(c), hspec, tile(oa.shape[2]), tile(ob.shape[2]), tile(3 * d), tile(d),
                  _resident(cw.shape), _resident((1, c)), _resident((1, c)), _resident((1, c)),
                  _resident(wc.shape), _resident(wa.shape), _resident(wb.shape), _resident(wo.shape)],
        out_specs=tile(d),
        scratch_shapes=[pltpu.VMEM((CONV_HIST + tm, c), F32),
                        pltpu.VMEM((SUBLANES - 1, CONV_HIST + tm - SUBLANES, c), F32)],
        compiler_params=_params(2),
        name="conv_merge",
    )(u, hist, oa, ob, gate, x, cw, vec(cb), vec(lg), vec(lb), wc, wa, wb, wo)


def kernel(x_prompt, x_sample, cache_a_k, cache_a_v, cache_b_k, cache_b_v, state_conv, w_in, b_gate,
           rel_bias, w_a_out, w_b_out, conv_w, conv_b, conv_ln_g, conv_ln_b, w_c_out, w_o, ln_ffn1,
           ffn1_w_gate, ffn1_w_up, ffn1_w_down, ln_mix, ln_ffn2, ffn2_w_gate, ffn2_w_up, ffn2_w_down,
           final_norm):
    depth = w_in.shape[0]
    nb, t, d = x_prompt.shape
    ns, ts, _ = x_sample.shape
    a_len = cache_a_k.shape[2]
    past = cache_b_k.shape[2]
    h_a, h_b = cache_a_k.shape[3], cache_b_k.shape[3]
    wa_width, wb_width = h_a * HEAD_DIM, h_b * HEAD_DIM
    c_conv = state_conv.shape[3]
    assert wa_width == wb_width == c_conv
    assert t % 512 == 0 and a_len % BAND_BLOCK == 0 and past % SB_BLOCK == 0
    assert a_len == A_PAST_CHUNKS * CHUNK and past % CHUNK == 0 and ts <= CHUNK and ts % 16 == 0
    assert t >= a_len

    bf = lambda a: a.astype(BF16)
    w_in16, wa16, wb16, wc16, wo16 = map(_to_bf16, (w_in, w_a_out, w_b_out, w_c_out, w_o))
    f1g, f1u, f1d = map(_to_bf16, (ffn1_w_gate, ffn1_w_up, ffn1_w_down))
    f2g, f2u, f2d = map(_to_bf16, (ffn2_w_gate, ffn2_w_up, ffn2_w_down))

    bias_p, bias_s = _band_bias(rel_bias, kv_len=a_len + ts)
    band_qoff_s = a_len // BAND_BLOCK
    sb_qoff_s = past // SB_BLOCK

    def padded_cache(cache, extra):
        flat = bf(cache.reshape(*cache.shape[:3], -1))
        return jnp.pad(flat, ((0, 0), (0, 0), (0, extra), (0, 0)))

    seq_a_k, seq_a_v = padded_cache(cache_a_k, BAND_BLOCK), padded_cache(cache_a_v, BAND_BLOCK)
    seq_b_k, seq_b_v = padded_cache(cache_b_k, SB_BLOCK), padded_cache(cache_b_v, SB_BLOCK)
    hist = jnp.pad(state_conv, ((0, 0), (0, 0), (CONV_HIST - (CONV_W - 1), 0), (0, 0)))

    xp = x_prompt.reshape(nb * t, d)
    xs = x_sample.reshape(ns * ts, d)
    p_state, s_state = [], []
    for l in range(depth):
        last = l == depth - 1
        mix_w = (conv_w[l], conv_b[l], conv_ln_g[l], conv_ln_b[l], wc16[l], wa16[l], wb16[l], wo16[l])

        xp = _ffn(xp, ln_ffn1[l], f1g[l], f1u[l], f1d[l], final_norm, final=False)
        (qa, ka32, va32, ka16, va16, qb, kb32, vb32, kb16, vb16, u, gate) = _inproj(
            xp, ln_mix[l], w_in16[l], b_gate[l], width=wa_width, seq_len=t)
        seq = lambda a: a.reshape(nb, t, a.shape[-1])
        oa = _band_attention(seq(qa), seq(ka16), seq(va16), bias_p[l], qoff=0)
        ob = _stick_breaking(seq(qb), seq(kb16), seq(vb16), qoff=0)
        u3 = seq(u)
        xp = _merge(u3, u3, oa, ob, seq(gate), seq(xp), *mix_w, tm=512, hist_from_u=True)
        xp = _ffn(xp.reshape(nb * t, d), ln_ffn2[l], f2g[l], f2u[l], f2d[l], final_norm, final=last)
        heads = lambda a, h: a.reshape(nb, h, HEAD_DIM, a.shape[-1]).transpose(0, 3, 1, 2)
        tail = lambda a, h: heads(a[:, :, t - a_len:], h)
        p_state.append((tail(ka32, h_a), tail(va32, h_a),
                        heads(kb32, h_b), heads(vb32, h_b), u3[:, t - (CONV_W - 1):]))

        xs = _ffn(xs, ln_ffn1[l], f1g[l], f1u[l], f1d[l], final_norm, final=False)
        (qa, ka32, va32, ka16, va16, qb, kb32, vb32, kb16, vb16, u, gate) = _inproj(
            xs, ln_mix[l], w_in16[l], b_gate[l], width=wa_width)
        seq = lambda a: a.reshape(ns, ts, a.shape[-1])
        with_new = lambda padded, at, new: padded[l].at[:, at:at + ts].set(seq(new))
        oa = _band_attention(seq(qa), with_new(seq_a_k, a_len, ka16),
                             with_new(seq_a_v, a_len, va16), bias_s[l], qoff=band_qoff_s)
        ob = _stick_breaking(seq(qb), with_new(seq_b_k, past, kb16),
                             with_new(seq_b_v, past, vb16), qoff=sb_qoff_s)
        u3 = seq(u)
        xs = _merge(u3, hist[l], oa, ob, seq(gate), seq(xs), *mix_w, tm=ts, hist_from_u=False)
        xs = _ffn(xs.reshape(ns * ts, d), ln_ffn2[l], f2g[l], f2u[l], f2d[l], final_norm, final=last)
        heads = lambda a, h: a.reshape(ns, ts, h, HEAD_DIM)
        up = jnp.concatenate([state_conv[l], u3], axis=1)
        s_state.append((heads(ka32, h_a), heads(va32, h_a), heads(kb32, h_b), heads(vb32, h_b),
                        up[:, ts:]))

    stack = lambda states, i: jnp.stack([s[i] for s in states], axis=0)
    return (xp.reshape(nb, t, d), xs.reshape(ns, ts, d),
            stack(p_state, 0), stack(p_state, 1), stack(p_state, 2), stack(p_state, 3), stack(p_state, 4),
            stack(s_state, 0), stack(s_state, 1), stack(s_state, 2), stack(s_state, 3), stack(s_state, 4))
```

```python
import functools

import numpy as np
import jax
import jax.numpy as jnp
from jax import lax
from jax.experimental import pallas as pl
from jax.experimental.pallas import tpu as pltpu

F32 = jnp.float32
BF16 = jnp.bfloat16

HEAD_DIM = 64
CHUNK = 64
A_PAST_CHUNKS = 8
REL_MIN = -(CHUNK - 1)
REL_MAX = 128
CONV_W = 31
RMS_EPS = 1e-6
LN_EPS = 1e-5
NEG_INF = -1e30
LOG2_E = 1.4426950408889634
Q_SCALE = HEAD_DIM ** -0.5 * LOG2_E

LANES = 128
SUBLANES = 8
MXU_EDGE = 256
BAND_BLOCK = 4 * CHUNK
BAND_NBLK = A_PAST_CHUNKS * CHUNK // BAND_BLOCK + 1
SB_BLOCK = 256
SB_HEADS = 4
SB_DEAD = -175.0
CONV_HIST = 32
VMEM_LIMIT = 56 * 1024 * 1024


def _params(n_axes, vmem=VMEM_LIMIT):
    return pltpu.CompilerParams(dimension_semantics=("arbitrary",) * n_axes, vmem_limit_bytes=vmem)


def _resident(shape):
    nd = len(shape)
    return pl.BlockSpec(shape, lambda *_: (0,) * nd, pipeline_mode=pl.Buffered(1))


def _row_tile(m, want):
    tm = want
    while tm > 16 and m % tm:
        tm //= 2
    assert m % tm == 0
    return tm


def _rms(x, g):
    return x * lax.rsqrt(jnp.mean(x * x, axis=-1, keepdims=True) + RMS_EPS) * g


def _sigmoid(x):
    return 1.0 / (1.0 + jnp.exp(-x))


def _dot(a, b):
    return jnp.dot(a, b, preferred_element_type=F32)


def _dot_nt(a, b):
    return lax.dot_general(a, b, (((1,), (1,)), ((), ())), preferred_element_type=F32)


CAST_BLOCK_BYTES = 4 * 1024 * 1024


def _cast_body(x_ref, o_ref):
    o_ref[...] = x_ref[...].astype(BF16)


def _to_bf16(w):
    cols = w.shape[-1]
    w2 = w.reshape(-1, cols)
    rows = w2.shape[0]
    want = 1024
    while want * cols * 4 > CAST_BLOCK_BYTES:
        want //= 2
    tr = _row_tile(rows, want)
    spec = pl.BlockSpec((tr, cols), lambda i: (i, 0))
    out = pl.pallas_call(
        _cast_body,
        out_shape=jax.ShapeDtypeStruct((rows, cols), BF16),
        grid=(rows // tr,),
        in_specs=[spec],
        out_specs=spec,
        compiler_params=_params(1),
        name="cast_bf16",
    )(w2)
    return out.reshape(w.shape)


def _cache_rows_body(x_ref, o_ref, *, real_tiles):
    @pl.when(pl.program_id(1) < real_tiles)
    def _():
        o_ref[0] = x_ref[0].T.astype(BF16)

    @pl.when(pl.program_id(1) >= real_tiles)
    def _():
        o_ref[0] = jnp.zeros(o_ref.shape[1:], BF16)


def _cache_rows(cache, extra):
    depth, n, t, heads, hd = cache.shape
    w = heads * hd
    tt = MXU_EDGE
    assert t % tt == 0 and extra % tt == 0
    x = cache.transpose(0, 1, 3, 4, 2).reshape(depth * n, w, t)
    out = pl.pallas_call(
        functools.partial(_cache_rows_body, real_tiles=t // tt),
        out_shape=jax.ShapeDtypeStruct((depth * n, t + extra, w), BF16),
        grid=(depth * n, (t + extra) // tt),
        in_specs=[pl.BlockSpec((1, w, tt), lambda s, i: (s, 0, jnp.minimum(i, t // tt - 1)))],
        out_specs=pl.BlockSpec((1, tt, w), lambda s, i: (s, i, 0)),
        compiler_params=_params(2),
        name="cache_rows",
    )(x)
    return out.reshape(depth, n, t + extra, w)


def _ffn_body(x_ref, g_ref, wg_ref, wu_ref, wd_ref, fg_ref, o_ref, *, cuts, final):
    x = x_ref[...]
    h = _rms(x, g_ref[...]).astype(BF16)
    acc = None
    for lo, hi in zip(cuts[:-1], cuts[1:]):
        a = _dot(h, wg_ref[:, lo:hi])
        b = _dot(h, wu_ref[:, lo:hi])
        act = (a * _sigmoid(a) * b).astype(BF16)
        d = _dot(act, wd_ref[lo:hi, :])
        acc = d if acc is None else acc + d
    y = x + 0.5 * acc
    if final:
        y = _rms(y, fg_ref[...])
    o_ref[...] = y


def _ffn(x, g, wg, wu, wd, fg, *, final):
    m, d = x.shape
    d_ff = wg.shape[1]
    tm = _row_tile(m, 512)
    assert d_ff % MXU_EDGE == 0
    cuts = (0, (d_ff // MXU_EDGE + 1) // 2 * MXU_EDGE, d_ff)
    row = pl.BlockSpec((tm, d), lambda i: (i, 0))
    return pl.pallas_call(
        functools.partial(_ffn_body, cuts=cuts, final=final),
        out_shape=jax.ShapeDtypeStruct((m, d), F32),
        grid=(m // tm,),
        in_specs=[row, _resident((1, d)), _resident((d, d_ff)), _resident((d, d_ff)),
                  _resident((d_ff, d)), _resident((1, d))],
        out_specs=row,
        compiler_params=_params(1),
        name="ffn_half",
    )(x, g.reshape(1, d), wg, wu, wd, fg.reshape(1, d))


def _inproj_body(x_ref, g_ref, bg_ref, *rest, width, d_model, time_minor):
    w_refs = rest[:-12]
    (qa_ref, ka32_ref, va32_ref, ka16_ref, va16_ref,
     qb_ref, kb32_ref, vb32_ref, kb16_ref, vb16_ref, u_ref, gate_ref) = rest[-12:]
    h = _rms(x_ref[...], g_ref[...]).astype(BF16)

    def mm(col, n):
        slab, at = divmod(col, d_model)
        assert at + n <= d_model
        return _dot(h, w_refs[slab][:, at:at + n])

    def put_state(ref, val):
        if time_minor:
            ref[0] = val.T
        else:
            ref[...] = val

    scale = Q_SCALE
    col = 0
    for q_ref, k32_ref, v32_ref, k16_ref, v16_ref in (
            (qa_ref, ka32_ref, va32_ref, ka16_ref, va16_ref),
            (qb_ref, kb32_ref, vb32_ref, kb16_ref, vb16_ref)):
        q_ref[...] = (mm(col, width) * scale).astype(BF16)
        k = mm(col + width, width)
        put_state(k32_ref, k)
        k16_ref[...] = k.astype(BF16)
        v = mm(col + 2 * width, width)
        put_state(v32_ref, v)
        v16_ref[...] = v.astype(BF16)
        col += 3 * width
    lin = mm(col, width)
    gt = mm(col + width, width)
    u_ref[...] = lin * _sigmoid(gt)
    col += 2 * width
    for j in range(3):
        graw = mm(col + j * d_model, d_model) + bg_ref[:, j * d_model:(j + 1) * d_model]
        gate_ref[:, j * d_model:(j + 1) * d_model] = _sigmoid(graw).astype(BF16)


def _inproj(x, g, w_in, b_gate, *, width, seq_len=None):
    m, d = x.shape
    n_cols = w_in.shape[1]
    tm = _row_tile(m, 512)
    row = lambda n: pl.BlockSpec((tm, n), lambda i: (i, 0))
    sds = lambda n, dt: jax.ShapeDtypeStruct((m, n), dt)
    if seq_len is None:
        state, state_spec = sds(width, F32), row(width)
    else:
        per_seq = seq_len // tm
        state = jax.ShapeDtypeStruct((m // seq_len, width, seq_len), F32)
        state_spec = pl.BlockSpec((1, width, tm), lambda i: (i // per_seq, 0, i % per_seq))
    mixer = [sds(width, BF16), state, state, sds(width, BF16), sds(width, BF16)]
    mixer_specs = [row(width), state_spec, state_spec, row(width), row(width)]
    n_slabs = n_cols // d
    assert n_cols % d == 0 and d % width == 0
    slabs = [pl.BlockSpec((d, d), functools.partial(lambda i, s: (0, s), s=s),
                          pipeline_mode=pl.Buffered(1)) for s in range(n_slabs)]
    return pl.pallas_call(
        functools.partial(_inproj_body, width=width, d_model=d, time_minor=seq_len is not None),
        out_shape=mixer + mixer + [sds(width, F32), sds(3 * d, BF16)],
        grid=(m // tm,),
        in_specs=[row(d), _resident((1, d)), _resident((1, 3 * d))] + slabs,
        out_specs=mixer_specs + mixer_specs + [row(width), row(3 * d)],
        compiler_params=_params(1),
        name="in_projection",
    )(x, g.reshape(1, d), b_gate.reshape(1, 3 * d), *([w_in] * n_slabs))


def _band_body(q_ref, *rest, qoff, n_pairs):
    k_refs, v_refs = rest[:n_pairs], rest[n_pairs:2 * n_pairs]
    bias_ref, o_ref = rest[2 * n_pairs:]
    rows = q_ref.shape[1]
    blk = pl.program_id(1) + qoff
    low = lax.broadcasted_iota(jnp.int32, (rows, LANES), 1) < HEAD_DIM
    first = [blk + j - (BAND_NBLK - 1) for j in range(BAND_NBLK)]
    starts = [pl.multiple_of(jnp.maximum(f, 0) * BAND_BLOCK, BAND_BLOCK) for f in first]

    def attend(hp, window_complete):
        lanes = slice(hp * LANES, (hp + 1) * LANES)
        q = q_ref[0, :, lanes].astype(F32)
        q2 = jnp.concatenate([jnp.where(low, q, 0.0), jnp.where(low, 0.0, q)], axis=0).astype(BF16)
        s = []
        for j in range(BAND_NBLK):
            bias = bias_ref[hp, j]
            if rows < BAND_BLOCK:
                bias = jnp.concatenate([bias[0:rows], bias[BAND_BLOCK:BAND_BLOCK + rows]], axis=0)
            sj = _dot_nt(q2, k_refs[hp][0, pl.ds(starts[j], BAND_BLOCK), :]) + bias
            if j < BAND_NBLK - 1 and not window_complete:
                sj = jnp.where(first[j] >= 0, sj, NEG_INF)
            s.append(sj)
        mx = functools.reduce(jnp.maximum, [jnp.max(sj, axis=-1, keepdims=True) for sj in s])
        p = [jnp.exp2(sj - mx) for sj in s]
        den = functools.reduce(jnp.add, [jnp.sum(pj, axis=-1, keepdims=True) for pj in p])
        o = functools.reduce(jnp.add, [
            _dot(pj.astype(BF16), v_refs[hp][0, pl.ds(starts[j], BAND_BLOCK), :])
            for j, pj in enumerate(p)])
        o = o * (1.0 / den)
        o_ref[0, :, lanes] = jnp.where(low, o[:rows], o[rows:]).astype(BF16)

    @pl.when(first[0] >= 0)
    def _():
        for hp in range(n_pairs):
            attend(hp, True)

    @pl.when(first[0] < 0)
    def _():
        for hp in range(n_pairs):
            attend(hp, False)


def _band_attention(q, k, v, bias, *, qoff):
    n, tq, w = q.shape
    tk = k.shape[1]
    n_pairs = w // LANES
    rows = min(tq, BAND_BLOCK)
    qspec = pl.BlockSpec((1, rows, w), lambda b, i: (b, i, 0))
    mode = dict(pipeline_mode=pl.Buffered(1)) if tq > BAND_BLOCK else {}
    kspecs = [pl.BlockSpec((1, tk, LANES), functools.partial(lambda b, i, hp: (b, 0, hp), hp=hp),
                           **mode) for hp in range(n_pairs)]
    return pl.pallas_call(
        functools.partial(_band_body, qoff=qoff, n_pairs=n_pairs),
        out_shape=jax.ShapeDtypeStruct((n, tq, w), BF16),
        grid=(n, tq // rows),
        in_specs=[qspec] + kspecs + kspecs + [_resident(bias.shape)],
        out_specs=qspec,
        compiler_params=_params(2),
        name="band_attention",
    )(q, *([k] * n_pairs), *([v] * n_pairs), bias)


def _band_bias(rel_bias, kv_len):
    nb = BAND_BLOCK
    r = np.arange(nb)[:, None]
    c = np.arange(nb)[None, :]
    ok = np.zeros((BAND_NBLK, nb, nb), bool)
    real = np.zeros((BAND_NBLK, nb, nb), bool)
    m = np.arange(2 * nb)
    diff = np.where(m < nb, -m, 2 * nb - m)
    line_idx = np.zeros((BAND_NBLK, 2 * nb), np.int32)
    for j in range(BAND_NBLK):
        line_idx[j] = np.clip(diff + nb * (BAND_NBLK - 1 - j), REL_MIN, REL_MAX) - REL_MIN
        dchunk = (A_PAST_CHUNKS + r // CHUNK) - (c // CHUNK + (nb // CHUNK) * j)
        ok[j] = (dchunk >= 0) & (dchunk <= A_PAST_CHUNKS)
        real[j] = np.broadcast_to((nb * j + c) < kv_len, (nb, nb))
    depth, heads = rel_bias.shape[:2]
    line = (rel_bias.astype(F32) * LOG2_E)[:, :, line_idx]
    line = line.reshape(depth, heads // 2, 2, BAND_NBLK, 2 * nb).transpose(0, 1, 3, 2, 4)
    table = jnp.tile(line, (1, 1, 1, 1, nb))[..., :nb * (2 * nb - 1)]
    table = table.reshape(depth, heads // 2, BAND_NBLK, 2, nb, 2 * nb - 1)[..., :nb]
    table = jnp.where(ok[None, None, :, None], table, NEG_INF)
    padded = jnp.where(real[None, None, :, None], table, NEG_INF)
    shape = (depth, heads // 2, BAND_NBLK, 2 * nb, nb)
    return table.reshape(shape), padded.reshape(shape)


def _sb_body(q_ref, k_ref, v_ref, tri_ref, o_ref, *, qoff):
    blk = pl.program_id(2) + qoff
    width = SB_HEADS * HEAD_DIM
    tq = q_ref.shape[1]
    rows = SB_HEADS * tq
    lane = lax.broadcasted_iota(jnp.int32, (tq, width), 1)
    q = q_ref[0].astype(F32)
    qs = jnp.concatenate(
        [jnp.where((lane >= h * HEAD_DIM) & (lane < (h + 1) * HEAD_DIM), q, 0.0)
         for h in range(SB_HEADS)], axis=0).astype(BF16)
    tri = tri_ref[...]
    row = lax.broadcasted_iota(jnp.int32, (rows, SB_BLOCK), 0) & (tq - 1)
    col = lax.broadcasted_iota(jnp.int32, (rows, SB_BLOCK), 1)
    causal = col < row
    low = lax.broadcasted_iota(jnp.int32, (tq, LANES), 1) < HEAD_DIM

    def sweep(kb, acc, later, diagonal, valid=None):
        start = pl.multiple_of(kb * SB_BLOCK, SB_BLOCK)
        k = k_ref[0, pl.ds(start, SB_BLOCK), :]
        v = v_ref[0, pl.ds(start, SB_BLOCK), :]
        z = _dot_nt(qs, k)
        drop = jnp.maximum(z, 0.0) + jnp.log2(1.0 + jnp.exp2(-jnp.abs(z)))
        if diagonal:
            drop = jnp.where(causal, drop, 0.0)
        suffix = _dot(drop.astype(BF16), tri)
        a = jnp.exp2(z + suffix + later)
        if diagonal:
            a = jnp.where(causal, a, 0.0)
        pv = _dot(a.astype(BF16), v)
        pieces = [pv[h * tq:(h + 1) * tq, (h // 2) * LANES:(h // 2 + 1) * LANES]
                  for h in range(SB_HEADS)]
        picked = jnp.concatenate([jnp.where(low, pieces[0], pieces[1]),
                                  jnp.where(low, pieces[2], pieces[3])], axis=1)
        total = suffix[:, 0:1]
        if valid is not None:
            picked = jnp.where(valid, picked, 0.0)
            total = jnp.where(valid, total, 0.0)
        return acc + picked, later + total

    acc = jnp.zeros((tq, width), F32)
    later = jnp.zeros((rows, 1), F32)
    acc, later = sweep(blk, acc, later, True)
    acc, later = sweep(jnp.maximum(blk - 1, 0), acc, later, False, valid=blk > 0)

    def body(carry):
        _, kb, acc, later = carry
        acc, later = sweep(kb, acc, later, False)
        alive = jnp.max(later) > SB_DEAD
        more = jnp.logical_and(kb > 0, alive).astype(jnp.int32)
        return more, kb - 1, acc, later

    start = jnp.logical_and(blk > 1, jnp.max(later) > SB_DEAD).astype(jnp.int32)
    _, _, acc, later = lax.while_loop(lambda c: c[0] > 0, body, (start, blk - 2, acc, later))
    o_ref[0] = acc.astype(BF16)


def _stick_breaking(q, k, v, *, qoff):
    n, tq, w = q.shape
    tk = k.shape[1]
    gw = SB_HEADS * HEAD_DIM
    rows = min(tq, SB_BLOCK)
    assert rows & (rows - 1) == 0
    tri = jnp.asarray(-np.tril(np.ones((SB_BLOCK, SB_BLOCK), np.float32)), BF16)
    qspec = pl.BlockSpec((1, rows, gw), lambda b, g, i: (b, i, g))
    kspec = pl.BlockSpec((1, tk, gw), lambda b, g, i: (b, 0, g))
    return pl.pallas_call(
        functools.partial(_sb_body, qoff=qoff),
        out_shape=jax.ShapeDtypeStruct((n, tq, w), BF16),
        grid=(n, w // gw, tq // rows),
        in_specs=[qspec, kspec, kspec, _resident((SB_BLOCK, SB_BLOCK))],
        out_specs=qspec,
        compiler_params=_params(3),
        name="stick_breaking",
    )(q, k, v, tri)


def _merge_body(u_ref, hist_ref, oa_ref, ob_ref, gate_ref, x_ref, cw_ref, cb_ref, lg_ref, lb_ref,
                wc_ref, wa_ref, wb_ref, wo_ref, o_ref, up_ref, rot_ref, *, zero_first_hist):
    tm = u_ref.shape[1]
    d = x_ref.shape[2]
    hist = hist_ref[0]
    if zero_first_hist:
        hist = jnp.where(pl.program_id(1) > 0, hist, 0.0)
    up_ref[0:CONV_HIST, :] = hist
    up_ref[CONV_HIST:CONV_HIST + tm, :] = u_ref[0]
    span = rot_ref.shape[1]
    for b in range(1, SUBLANES):
        rot_ref[b - 1] = up_ref[b:b + span, :]
    first = CONV_HIST - (CONV_W - 1)

    def rows_out(r0, n):
        y = None
        for w in range(CONV_W):
            a, b = divmod(first + w, SUBLANES)
            lo = SUBLANES * a + r0
            window = up_ref[lo:lo + n, :] if b == 0 else rot_ref[b - 1, lo:lo + n, :]
            term = window * cw_ref[w:w + 1, :]
            y = term if y is None else y + term
        y = y + cb_ref[...]
        mu = jnp.mean(y, axis=-1, keepdims=True)
        yc = y - mu
        var = jnp.mean(yc * yc, axis=-1, keepdims=True)
        yn = yc * lax.rsqrt(var + LN_EPS) * lg_ref[...] + lb_ref[...]
        oc = _dot((yn * _sigmoid(yn)).astype(BF16), wc_ref[...])
        pa = _dot(oa_ref[0, r0:r0 + n, :], wa_ref[...])
        pb = _dot(ob_ref[0, r0:r0 + n, :], wb_ref[...])
        merged = (gate_ref[0, r0:r0 + n, 0:d].astype(F32) * pa
                  + gate_ref[0, r0:r0 + n, d:2 * d].astype(F32) * pb
                  + gate_ref[0, r0:r0 + n, 2 * d:3 * d].astype(F32) * oc)
        o_ref[0, r0:r0 + n, :] = x_ref[0, r0:r0 + n, :] + _dot(merged.astype(BF16), wo_ref[...])

    halves = 2 if tm % (2 * MXU_EDGE) == 0 else 1
    for part in range(halves):
        rows_out(part * (tm // halves), tm // halves)


def _merge(u, hist, oa, ob, gate, x, cw, cb, lg, lb, wc, wa, wb, wo, *, tm, hist_from_u):
    n, t, c = u.shape
    d = x.shape[2]
    tile = lambda ch: pl.BlockSpec((1, tm, ch), lambda b, i: (b, i, 0))
    if hist_from_u:
        per = tm // CONV_HIST
        hspec = pl.BlockSpec((1, CONV_HIST, c), lambda b, i: (b, jnp.maximum(i * per - 1, 0), 0))
    else:
        hspec = pl.BlockSpec((1, CONV_HIST, c), lambda b, i: (b, 0, 0))
    vec = lambda a: a.reshape(1, -1)
    return pl.pallas_call(
        functools.partial(_merge_body, zero_first_hist=hist_from_u),
        out_shape=jax.ShapeDtypeStruct((n, t, d), F32),
        grid=(n, t // tm),
        in_specs=[tile(c), hspec, tile(oa.shape[2]), tile(ob.shape[2]), tile(3 * d), tile(d),
                  _resident(cw.shape), _resident((1, c)), _resident((1, c)), _resident((1, c)),
                  _resident(wc.shape), _resident(wa.shape), _resident(wb.shape), _resident(wo.shape)],
        out_specs=tile(d),
        scratch_shapes=[pltpu.VMEM((CONV_HIST + tm, c), F32),
                        pltpu.VMEM((SUBLANES - 1, CONV_HIST + tm - SUBLANES, c), F32)],
        compiler_params=_params(2),
        name="conv_merge",
    )(u, hist, oa, ob, gate, x, cw, vec(cb), vec(lg), vec(lb), wc, wa, wb, wo)


def kernel(x_prompt, x_sample, cache_a_k, cache_a_v, cache_b_k, cache_b_v, state_conv, w_in, b_gate,
           rel_bias, w_a_out, w_b_out, conv_w, conv_b, conv_ln_g, conv_ln_b, w_c_out, w_o, ln_ffn1,
           ffn1_w_gate, ffn1_w_up, ffn1_w_down, ln_mix, ln_ffn2, ffn2_w_gate, ffn2_w_up, ffn2_w_down,
           final_norm):
    depth = w_in.shape[0]
    nb, t, d = x_prompt.shape
    ns, ts, _ = x_sample.shape
    a_len = cache_a_k.shape[2]
    past = cache_b_k.shape[2]
    h_a, h_b = cache_a_k.shape[3], cache_b_k.shape[3]
    wa_width, wb_width = h_a * HEAD_DIM, h_b * HEAD_DIM
    c_conv = state_conv.shape[3]
    assert wa_width == wb_width == c_conv
    assert t % 512 == 0 and a_len % BAND_BLOCK == 0 and past % SB_BLOCK == 0
    assert a_len == A_PAST_CHUNKS * CHUNK and past % CHUNK == 0 and ts <= CHUNK and ts % 16 == 0
    assert t >= a_len

    w_in16, wa16, wb16, wc16, wo16 = map(_to_bf16, (w_in, w_a_out, w_b_out, w_c_out, w_o))
    f1g, f1u, f1d = map(_to_bf16, (ffn1_w_gate, ffn1_w_up, ffn1_w_down))
    f2g, f2u, f2d = map(_to_bf16, (ffn2_w_gate, ffn2_w_up, ffn2_w_down))

    bias_p, bias_s = _band_bias(rel_bias, kv_len=a_len + ts)
    band_qoff_s = a_len // BAND_BLOCK
    sb_qoff_s = past // SB_BLOCK

    seq_a_k, seq_a_v = _cache_rows(cache_a_k, BAND_BLOCK), _cache_rows(cache_a_v, BAND_BLOCK)
    seq_b_k, seq_b_v = _cache_rows(cache_b_k, SB_BLOCK), _cache_rows(cache_b_v, SB_BLOCK)
    hist = jnp.pad(state_conv, ((0, 0), (0, 0), (CONV_HIST - (CONV_W - 1), 0), (0, 0)))

    xp = x_prompt.reshape(nb * t, d)
    xs = x_sample.reshape(ns * ts, d)
    p_state, s_state = [], []
    for l in range(depth):
        last = l == depth - 1
        mix_w = (conv_w[l], conv_b[l], conv_ln_g[l], conv_ln_b[l], wc16[l], wa16[l], wb16[l], wo16[l])

        xp = _ffn(xp, ln_ffn1[l], f1g[l], f1u[l], f1d[l], final_norm, final=False)
        (qa, ka32, va32, ka16, va16, qb, kb32, vb32, kb16, vb16, u, gate) = _inproj(
            xp, ln_mix[l], w_in16[l], b_gate[l], width=wa_width, seq_len=t)
        seq = lambda a: a.reshape(nb, t, a.shape[-1])
        oa = _band_attention(seq(qa), seq(ka16), seq(va16), bias_p[l], qoff=0)
        ob = _stick_breaking(seq(qb), seq(kb16), seq(vb16), qoff=0)
        u3 = seq(u)
        xp = _merge(u3, u3, oa, ob, seq(gate), seq(xp), *mix_w, tm=512, hist_from_u=True)
        xp = _ffn(xp.reshape(nb * t, d), ln_ffn2[l], f2g[l], f2u[l], f2d[l], final_norm, final=last)
        heads = lambda a, h: a.reshape(nb, h, HEAD_DIM, a.shape[-1]).transpose(0, 3, 1, 2)
        tail = lambda a, h: heads(a[:, :, t - a_len:], h)
        p_state.append((tail(ka32, h_a), tail(va32, h_a),
                        heads(kb32, h_b), heads(vb32, h_b), u3[:, t - (CONV_W - 1):]))

        xs = _ffn(xs, ln_ffn1[l], f1g[l], f1u[l], f1d[l], final_norm, final=False)
        (qa, ka32, va32, ka16, va16, qb, kb32, vb32, kb16, vb16, u, gate) = _inproj(
            xs, ln_mix[l], w_in16[l], b_gate[l], width=wa_width)
        seq = lambda a: a.reshape(ns, ts, a.shape[-1])
        with_new = lambda padded, at, new: padded[l].at[:, at:at + ts].set(seq(new))
        oa = _band_attention(seq(qa), with_new(seq_a_k, a_len, ka16),
                             with_new(seq_a_v, a_len, va16), bias_s[l], qoff=band_qoff_s)
        ob = _stick_breaking(seq(qb), with_new(seq_b_k, past, kb16),
                             with_new(seq_b_v, past, vb16), qoff=sb_qoff_s)
        u3 = seq(u)
        xs = _merge(u3, hist[l], oa, ob, seq(gate), seq(xs), *mix_w, tm=ts, hist_from_u=False)
        xs = _ffn(xs.reshape(ns * ts, d), ln_ffn2[l], f2g[l], f2u[l], f2d[l], final_norm, final=last)
        heads = lambda a, h: a.reshape(ns, ts, h, HEAD_DIM)
        up = jnp.concatenate([state_conv[l], u3], axis=1)
        s_state.append((heads(ka32, h_a), heads(va32, h_a), heads(kb32, h_b), heads(vb32, h_b),
                        up[:, ts:]))

    stack = lambda states, i: jnp.stack([s[i] for s in states], axis=0)
    return (xp.reshape(nb, t, d), xs.reshape(ns, ts, d),
            stack(p_state, 0), stack(p_state, 1), stack(p_state, 2), stack(p_state, 3), stack(p_state, 4),
            stack(s_state, 0), stack(s_state, 1), stack(s_state, 2), stack(s_state, 3), stack(s_state, 4))
```

```python
import functools

import numpy as np
import jax
import jax.numpy as jnp
from jax import lax
from jax.experimental import pallas as pl
from jax.experimental.pallas import tpu as pltpu

F32 = jnp.float32
BF16 = jnp.bfloat16

HEAD_DIM = 64
CHUNK = 64
A_PAST_CHUNKS = 8
REL_MIN = -(CHUNK - 1)
REL_MAX = 128
CONV_W = 31
RMS_EPS = 1e-6
LN_EPS = 1e-5
NEG_INF = -1e30
LOG2_E = 1.4426950408889634
Q_SCALE = HEAD_DIM ** -0.5 * LOG2_E

LANES = 128
SUBLANES = 8
MXU_EDGE = 256
BAND_BLOCK = 4 * CHUNK
BAND_NBLK = A_PAST_CHUNKS * CHUNK // BAND_BLOCK + 1
SB_BLOCK = 256
SB_HEADS = 4
SB_DEAD = -175.0
CONV_HIST = 32
VMEM_LIMIT = 56 * 1024 * 1024


def _params(n_axes, vmem=VMEM_LIMIT):
    return pltpu.CompilerParams(dimension_semantics=("arbitrary",) * n_axes, vmem_limit_bytes=vmem)


def _resident(shape):
    nd = len(shape)
    return pl.BlockSpec(shape, lambda *_: (0,) * nd, pipeline_mode=pl.Buffered(1))


def _row_tile(m, want):
    tm = want
    while tm > 16 and m % tm:
        tm //= 2
    assert m % tm == 0
    return tm


def _rms(x, g):
    return x * lax.rsqrt(jnp.mean(x * x, axis=-1, keepdims=True) + RMS_EPS) * g


def _sigmoid(x):
    return 1.0 / (1.0 + jnp.exp(-x))


def _dot(a, b):
    return jnp.dot(a, b, preferred_element_type=F32)


def _dot_nt(a, b):
    return lax.dot_general(a, b, (((1,), (1,)), ((), ())), preferred_element_type=F32)


CAST_BLOCK_BYTES = 4 * 1024 * 1024


def _cast_body(x_ref, o_ref):
    o_ref[...] = x_ref[...].astype(BF16)


def _to_bf16(w):
    cols = w.shape[-1]
    w2 = w.reshape(-1, cols)
    rows = w2.shape[0]
    want = 1024
    while want * cols * 4 > CAST_BLOCK_BYTES:
        want //= 2
    tr = _row_tile(rows, want)
    spec = pl.BlockSpec((tr, cols), lambda i: (i, 0))
    out = pl.pallas_call(
        _cast_body,
        out_shape=jax.ShapeDtypeStruct((rows, cols), BF16),
        grid=(rows // tr,),
        in_specs=[spec],
        out_specs=spec,
        compiler_params=_params(1),
        name="cast_bf16",
    )(w2)
    return out.reshape(w.shape)


def _cache_rows_body(x_ref, o_ref):
    t = x_ref.shape[2]
    for lo in range(0, t, MXU_EDGE):
        o_ref[0, lo:lo + MXU_EDGE, :] = x_ref[0, :, lo:lo + MXU_EDGE].T.astype(BF16)
    o_ref[0, t:, :] = jnp.zeros((o_ref.shape[1] - t, o_ref.shape[2]), BF16)


def _cache_rows(cache, extra):
    depth, n, t, heads, hd = cache.shape
    w = heads * hd
    assert t % MXU_EDGE == 0 and extra % SUBLANES == 0
    x = cache.transpose(0, 1, 3, 4, 2).reshape(depth * n, w, t)
    out = pl.pallas_call(
        _cache_rows_body,
        out_shape=jax.ShapeDtypeStruct((depth * n, t + extra, w), BF16),
        grid=(depth * n,),
        in_specs=[pl.BlockSpec((1, w, t), lambda s: (s, 0, 0))],
        out_specs=pl.BlockSpec((1, t + extra, w), lambda s: (s, 0, 0)),
        compiler_params=_params(1),
        name="cache_rows",
    )(x)
    return out.reshape(depth, n, t + extra, w)


def _ffn_body(x_ref, g_ref, wg_ref, wu_ref, wd_ref, fg_ref, o_ref, *, cuts, final):
    x = x_ref[...]
    h = _rms(x, g_ref[...]).astype(BF16)
    acc = None
    for lo, hi in zip(cuts[:-1], cuts[1:]):
        a = _dot(h, wg_ref[:, lo:hi])
        b = _dot(h, wu_ref[:, lo:hi])
        act = (a * _sigmoid(a) * b).astype(BF16)
        d = _dot(act, wd_ref[lo:hi, :])
        acc = d if acc is None else acc + d
    y = x + 0.5 * acc
    if final:
        y = _rms(y, fg_ref[...])
    o_ref[...] = y


def _ffn(x, g, wg, wu, wd, fg, *, final):
    m, d = x.shape
    d_ff = wg.shape[1]
    tm = _row_tile(m, 512)
    assert d_ff % MXU_EDGE == 0
    cuts = (0, (d_ff // MXU_EDGE + 1) // 2 * MXU_EDGE, d_ff)
    row = pl.BlockSpec((tm, d), lambda i: (i, 0))
    return pl.pallas_call(
        functools.partial(_ffn_body, cuts=cuts, final=final),
        out_shape=jax.ShapeDtypeStruct((m, d), F32),
        grid=(m // tm,),
        in_specs=[row, _resident((1, d)), _resident((d, d_ff)), _resident((d, d_ff)),
                  _resident((d_ff, d)), _resident((1, d))],
        out_specs=row,
        compiler_params=_params(1),
        name="ffn_half",
    )(x, g.reshape(1, d), wg, wu, wd, fg.reshape(1, d))


def _inproj_body(x_ref, g_ref, bg_ref, *rest, width, d_model, time_minor):
    w_refs = rest[:-12]
    (qa_ref, ka32_ref, va32_ref, ka16_ref, va16_ref,
     qb_ref, kb32_ref, vb32_ref, kb16_ref, vb16_ref, u_ref, gate_ref) = rest[-12:]
    h = _rms(x_ref[...], g_ref[...]).astype(BF16)

    def mm(col, n):
        slab, at = divmod(col, d_model)
        assert at + n <= d_model
        return _dot(h, w_refs[slab][:, at:at + n])

    def put_state(ref, val):
        if time_minor:
            ref[0] = val.T
        else:
            ref[...] = val

    scale = Q_SCALE
    col = 0
    for q_ref, k32_ref, v32_ref, k16_ref, v16_ref in (
            (qa_ref, ka32_ref, va32_ref, ka16_ref, va16_ref),
            (qb_ref, kb32_ref, vb32_ref, kb16_ref, vb16_ref)):
        q_ref[...] = (mm(col, width) * scale).astype(BF16)
        k = mm(col + width, width)
        put_state(k32_ref, k)
        k16_ref[...] = k.astype(BF16)
        v = mm(col + 2 * width, width)
        put_state(v32_ref, v)
        v16_ref[...] = v.astype(BF16)
        col += 3 * width
    lin = mm(col, width)
    gt = mm(col + width, width)
    u_ref[...] = lin * _sigmoid(gt)
    col += 2 * width
    for j in range(3):
        graw = mm(col + j * d_model, d_model) + bg_ref[:, j * d_model:(j + 1) * d_model]
        gate_ref[:, j * d_model:(j + 1) * d_model] = _sigmoid(graw).astype(BF16)


def _inproj(x, g, w_in, b_gate, *, width, seq_len=None):
    m, d = x.shape
    n_cols = w_in.shape[1]
    tm = _row_tile(m, 512)
    row = lambda n: pl.BlockSpec((tm, n), lambda i: (i, 0))
    sds = lambda n, dt: jax.ShapeDtypeStruct((m, n), dt)
    if seq_len is None:
        state, state_spec = sds(width, F32), row(width)
    else:
        per_seq = seq_len // tm
        state = jax.ShapeDtypeStruct((m // seq_len, width, seq_len), F32)
        state_spec = pl.BlockSpec((1, width, tm), lambda i: (i // per_seq, 0, i % per_seq))
    mixer = [sds(width, BF16), state, state, sds(width, BF16), sds(width, BF16)]
    mixer_specs = [row(width), state_spec, state_spec, row(width), row(width)]
    n_slabs = n_cols // d
    assert n_cols % d == 0 and d % width == 0
    slabs = [pl.BlockSpec((d, d), functools.partial(lambda i, s: (0, s), s=s),
                          pipeline_mode=pl.Buffered(1)) for s in range(n_slabs)]
    return pl.pallas_call(
        functools.partial(_inproj_body, width=width, d_model=d, time_minor=seq_len is not None),
        out_shape=mixer + mixer + [sds(width, F32), sds(3 * d, BF16)],
        grid=(m // tm,),
        in_specs=[row(d), _resident((1, d)), _resident((1, 3 * d))] + slabs,
        out_specs=mixer_specs + mixer_specs + [row(width), row(3 * d)],
        compiler_params=_params(1),
        name="in_projection",
    )(x, g.reshape(1, d), b_gate.reshape(1, 3 * d), *([w_in] * n_slabs))


def _band_body(q_ref, *rest, qoff, n_pairs):
    k_refs, v_refs = rest[:n_pairs], rest[n_pairs:2 * n_pairs]
    bias_ref, o_ref = rest[2 * n_pairs:]
    rows = q_ref.shape[1]
    blk = pl.program_id(1) + qoff
    low = lax.broadcasted_iota(jnp.int32, (rows, LANES), 1) < HEAD_DIM
    first = [blk + j - (BAND_NBLK - 1) for j in range(BAND_NBLK)]
    starts = [pl.multiple_of(jnp.maximum(f, 0) * BAND_BLOCK, BAND_BLOCK) for f in first]

    def attend(hp, window_complete):
        lanes = slice(hp * LANES, (hp + 1) * LANES)
        q = q_ref[0, :, lanes].astype(F32)
        q2 = jnp.concatenate([jnp.where(low, q, 0.0), jnp.where(low, 0.0, q)], axis=0).astype(BF16)
        s = []
        for j in range(BAND_NBLK):
            bias = bias_ref[hp, j]
            if rows < BAND_BLOCK:
                bias = jnp.concatenate([bias[0:rows], bias[BAND_BLOCK:BAND_BLOCK + rows]], axis=0)
            sj = _dot_nt(q2, k_refs[hp][0, pl.ds(starts[j], BAND_BLOCK), :]) + bias
            if j < BAND_NBLK - 1 and not window_complete:
                sj = jnp.where(first[j] >= 0, sj, NEG_INF)
            s.append(sj)
        mx = functools.reduce(jnp.maximum, [jnp.max(sj, axis=-1, keepdims=True) for sj in s])
        p = [jnp.exp2(sj - mx) for sj in s]
        den = functools.reduce(jnp.add, [jnp.sum(pj, axis=-1, keepdims=True) for pj in p])
        o = functools.reduce(jnp.add, [
            _dot(pj.astype(BF16), v_refs[hp][0, pl.ds(starts[j], BAND_BLOCK), :])
            for j, pj in enumerate(p)])
        o = o * (1.0 / den)
        o_ref[0, :, lanes] = jnp.where(low, o[:rows], o[rows:]).astype(BF16)

    @pl.when(first[0] >= 0)
    def _():
        for hp in range(n_pairs):
            attend(hp, True)

    @pl.when(first[0] < 0)
    def _():
        for hp in range(n_pairs):
            attend(hp, False)


def _band_attention(q, k, v, bias, *, qoff):
    n, tq, w = q.shape
    tk = k.shape[1]
    n_pairs = w // LANES
    rows = min(tq, BAND_BLOCK)
    qspec = pl.BlockSpec((1, rows, w), lambda b, i: (b, i, 0))
    mode = dict(pipeline_mode=pl.Buffered(1)) if tq > BAND_BLOCK else {}
    kspecs = [pl.BlockSpec((1, tk, LANES), functools.partial(lambda b, i, hp: (b, 0, hp), hp=hp),
                           **mode) for hp in range(n_pairs)]
    return pl.pallas_call(
        functools.partial(_band_body, qoff=qoff, n_pairs=n_pairs),
        out_shape=jax.ShapeDtypeStruct((n, tq, w), BF16),
        grid=(n, tq // rows),
        in_specs=[qspec] + kspecs + kspecs + [_resident(bias.shape)],
        out_specs=qspec,
        compiler_params=_params(2),
        name="band_attention",
    )(q, *([k] * n_pairs), *([v] * n_pairs), bias)


def _band_bias(rel_bias, kv_len):
    nb = BAND_BLOCK
    r = np.arange(nb)[:, None]
    c = np.arange(nb)[None, :]
    ok = np.zeros((BAND_NBLK, nb, nb), bool)
    real = np.zeros((BAND_NBLK, nb, nb), bool)
    m = np.arange(2 * nb)
    diff = np.where(m < nb, -m, 2 * nb - m)
    line_idx = np.zeros((BAND_NBLK, 2 * nb), np.int32)
    for j in range(BAND_NBLK):
        line_idx[j] = np.clip(diff + nb * (BAND_NBLK - 1 - j), REL_MIN, REL_MAX) - REL_MIN
        dchunk = (A_PAST_CHUNKS + r // CHUNK) - (c // CHUNK + (nb // CHUNK) * j)
        ok[j] = (dchunk >= 0) & (dchunk <= A_PAST_CHUNKS)
        real[j] = np.broadcast_to((nb * j + c) < kv_len, (nb, nb))
    depth, heads = rel_bias.shape[:2]
    line = (rel_bias.astype(F32) * LOG2_E)[:, :, line_idx]
    line = line.reshape(depth, heads // 2, 2, BAND_NBLK, 2 * nb).transpose(0, 1, 3, 2, 4)
    table = jnp.tile(line, (1, 1, 1, 1, nb))[..., :nb * (2 * nb - 1)]
    table = table.reshape(depth, heads // 2, BAND_NBLK, 2, nb, 2 * nb - 1)[..., :nb]
    table = jnp.where(ok[None, None, :, None], table, NEG_INF)
    padded = jnp.where(real[None, None, :, None], table, NEG_INF)
    shape = (depth, heads // 2, BAND_NBLK, 2 * nb, nb)
    return table.reshape(shape), padded.reshape(shape)


def _sb_body(q_ref, k_ref, v_ref, tri_ref, o_ref, *, qoff):
    blk = pl.program_id(2) + qoff
    width = SB_HEADS * HEAD_DIM
    tq = q_ref.shape[1]
    rows = SB_HEADS * tq
    lane = lax.broadcasted_iota(jnp.int32, (tq, width), 1)
    q = q_ref[0].astype(F32)
    qs = jnp.concatenate(
        [jnp.where((lane >= h * HEAD_DIM) & (lane < (h + 1) * HEAD_DIM), q, 0.0)
         for h in range(SB_HEADS)], axis=0).astype(BF16)
    tri = tri_ref[...]
    row = lax.broadcasted_iota(jnp.int32, (rows, SB_BLOCK), 0) & (tq - 1)
    col = lax.broadcasted_iota(jnp.int32, (rows, SB_BLOCK), 1)
    causal = col < row
    low = lax.broadcasted_iota(jnp.int32, (tq, LANES), 1) < HEAD_DIM

    def sweep(kb, acc, later, diagonal, valid=None):
        start = pl.multiple_of(kb * SB_BLOCK, SB_BLOCK)
        k = k_ref[0, pl.ds(start, SB_BLOCK), :]
        v = v_ref[0, pl.ds(start, SB_BLOCK), :]
        z = _dot_nt(qs, k)
        drop = jnp.maximum(z, 0.0) + jnp.log2(1.0 + jnp.exp2(-jnp.abs(z)))
        if diagonal:
            drop = jnp.where(causal, drop, 0.0)
        suffix = _dot(drop.astype(BF16), tri)
        a = jnp.exp2(z + suffix + later)
        if diagonal:
            a = jnp.where(causal, a, 0.0)
        pv = _dot(a.astype(BF16), v)
        pieces = [pv[h * tq:(h + 1) * tq, (h // 2) * LANES:(h // 2 + 1) * LANES]
                  for h in range(SB_HEADS)]
        picked = jnp.concatenate([jnp.where(low, pieces[0], pieces[1]),
                                  jnp.where(low, pieces[2], pieces[3])], axis=1)
        total = suffix[:, 0:1]
        if valid is not None:
            picked = jnp.where(valid, picked, 0.0)
            total = jnp.where(valid, total, 0.0)
        return acc + picked, later + total

    acc = jnp.zeros((tq, width), F32)
    later = jnp.zeros((rows, 1), F32)
    acc, later = sweep(blk, acc, later, True)
    acc, later = sweep(jnp.maximum(blk - 1, 0), acc, later, False, valid=blk > 0)

    def body(carry):
        _, kb, acc, later = carry
        acc, later = sweep(kb, acc, later, False)
        alive = jnp.max(later) > SB_DEAD
        more = jnp.logical_and(kb > 0, alive).astype(jnp.int32)
        return more, kb - 1, acc, later

    start = jnp.logical_and(blk > 1, jnp.max(later) > SB_DEAD).astype(jnp.int32)
    _, _, acc, later = lax.while_loop(lambda c: c[0] > 0, body, (start, blk - 2, acc, later))
    o_ref[0] = acc.astype(BF16)


def _stick_breaking(q, k, v, *, qoff):
    n, tq, w = q.shape
    tk = k.shape[1]
    gw = SB_HEADS * HEAD_DIM
    rows = min(tq, SB_BLOCK)
    assert rows & (rows - 1) == 0
    tri = jnp.asarray(-np.tril(np.ones((SB_BLOCK, SB_BLOCK), np.float32)), BF16)
    qspec = pl.BlockSpec((1, rows, gw), lambda b, g, i: (b, i, g))
    kspec = pl.BlockSpec((1, tk, gw), lambda b, g, i: (b, 0, g))
    return pl.pallas_call(
        functools.partial(_sb_body, qoff=qoff),
        out_shape=jax.ShapeDtypeStruct((n, tq, w), BF16),
        grid=(n, w // gw, tq // rows),
        in_specs=[qspec, kspec, kspec, _resident((SB_BLOCK, SB_BLOCK))],
        out_specs=qspec,
        compiler_params=_params(3),
        name="stick_breaking",
    )(q, k, v, tri)


def _merge_body(u_ref, hist_ref, oa_ref, ob_ref, gate_ref, x_ref, cw_ref, cb_ref, lg_ref, lb_ref,
                wc_ref, wa_ref, wb_ref, wo_ref, o_ref, up_ref, rot_ref, *, zero_first_hist):
    tm = u_ref.shape[1]
    d = x_ref.shape[2]
    hist = hist_ref[0]
    if zero_first_hist:
        hist = jnp.where(pl.program_id(1) > 0, hist, 0.0)
    up_ref[0:CONV_HIST, :] = hist
    up_ref[CONV_HIST:CONV_HIST + tm, :] = u_ref[0]
    span = rot_ref.shape[1]
    for b in range(1, SUBLANES):
        rot_ref[b - 1] = up_ref[b:b + span, :]
    first = CONV_HIST - (CONV_W - 1)

    def rows_out(r0, n):
        y = None
        for w in range(CONV_W):
            a, b = divmod(first + w, SUBLANES)
            lo = SUBLANES * a + r0
            window = up_ref[lo:lo + n, :] if b == 0 else rot_ref[b - 1, lo:lo + n, :]
            term = window * cw_ref[w:w + 1, :]
            y = term if y is None else y + term
        y = y + cb_ref[...]
        mu = jnp.mean(y, axis=-1, keepdims=True)
        yc = y - mu
        var = jnp.mean(yc * yc, axis=-1, keepdims=True)
        yn = yc * lax.rsqrt(var + LN_EPS) * lg_ref[...] + lb_ref[...]
        oc = _dot((yn * _sigmoid(yn)).astype(BF16), wc_ref[...])
        pa = _dot(oa_ref[0, r0:r0 + n, :], wa_ref[...])
        pb = _dot(ob_ref[0, r0:r0 + n, :], wb_ref[...])
        merged = (gate_ref[0, r0:r0 + n, 0:d].astype(F32) * pa
                  + gate_ref[0, r0:r0 + n, d:2 * d].astype(F32) * pb
                  + gate_ref[0, r0:r0 + n, 2 * d:3 * d].astype(F32) * oc)
        o_ref[0, r0:r0 + n, :] = x_ref[0, r0:r0 + n, :] + _dot(merged.astype(BF16), wo_ref[...])

    halves = 2 if tm % (2 * MXU_EDGE) == 0 else 1
    for part in range(halves):
        rows_out(part * (tm // halves), tm // halves)


def _merge(u, hist, oa, ob, gate, x, cw, cb, lg, lb, wc, wa, wb, wo, *, tm, hist_from_u):
    n, t, c = u.shape
    d = x.shape[2]
    tile = lambda ch: pl.BlockSpec((1, tm, ch), lambda b, i: (b, i, 0))
    if hist_from_u:
        per = tm // CONV_HIST
        hspec = pl.BlockSpec((1, CONV_HIST, c), lambda b, i: (b, jnp.maximum(i * per - 1, 0), 0))
    else:
        hspec = pl.BlockSpec((1, CONV_HIST, c), lambda b, i: (b, 0, 0))
    vec = lambda a: a.reshape(1, -1)
    return pl.pallas_call(
        functools.partial(_merge_body, zero_first_hist=hist_from_u),
        out_shape=jax.ShapeDtypeStruct((n, t, d), F32),
        grid=(n, t // tm),
        in_specs=[tile(c), hspec, tile(oa.shape[2]), tile(ob.shape[2]), tile(3 * d), tile(d),
                  _resident(cw.shape), _resident((1, c)), _resident((1, c)), _resident((1, c)),
                  _resident(wc.shape), _resident(wa.shape), _resident(wb.shape), _resident(wo.shape)],
        out_specs=tile(d),
        scratch_shapes=[pltpu.VMEM((CONV_HIST + tm, c), F32),
                        pltpu.VMEM((SUBLANES - 1, CONV_HIST + tm - SUBLANES, c), F32)],
        compiler_params=_params(2),
        name="conv_merge",
    )(u, hist, oa, ob, gate, x, cw, vec(cb), vec(lg), vec(lb), wc, wa, wb, wo)


def kernel(x_prompt, x_sample, cache_a_k, cache_a_v, cache_b_k, cache_b_v, state_conv, w_in, b_gate,
           rel_bias, w_a_out, w_b_out, conv_w, conv_b, conv_ln_g, conv_ln_b, w_c_out, w_o, ln_ffn1,
           ffn1_w_gate, ffn1_w_up, ffn1_w_down, ln_mix, ln_ffn2, ffn2_w_gate, ffn2_w_up, ffn2_w_down,
           final_norm):
    depth = w_in.shape[0]
    nb, t, d = x_prompt.shape
    ns, ts, _ = x_sample.shape
    a_len = cache_a_k.shape[2]
    past = cache_b_k.shape[2]
    h_a, h_b = cache_a_k.shape[3], cache_b_k.shape[3]
    wa_width, wb_width = h_a * HEAD_DIM, h_b * HEAD_DIM
    c_conv = state_conv.shape[3]
    assert wa_width == wb_width == c_conv
    assert t % 512 == 0 and a_len % BAND_BLOCK == 0 and past % SB_BLOCK == 0
    assert a_len == A_PAST_CHUNKS * CHUNK and past % CHUNK == 0 and ts <= CHUNK and ts % 16 == 0
    assert t >= a_len

    w_in16, wa16, wb16, wc16, wo16 = map(_to_bf16, (w_in, w_a_out, w_b_out, w_c_out, w_o))
    f1g, f1u, f1d = map(_to_bf16, (ffn1_w_gate, ffn1_w_up, ffn1_w_down))
    f2g, f2u, f2d = map(_to_bf16, (ffn2_w_gate, ffn2_w_up, ffn2_w_down))

    bias_p, bias_s = _band_bias(rel_bias, kv_len=a_len + ts)
    band_qoff_s = a_len // BAND_BLOCK
    sb_qoff_s = past // SB_BLOCK

    seq_a_k, seq_a_v = _cache_rows(cache_a_k, BAND_BLOCK), _cache_rows(cache_a_v, BAND_BLOCK)
    seq_b_k, seq_b_v = _cache_rows(cache_b_k, SB_BLOCK), _cache_rows(cache_b_v, SB_BLOCK)
    hist = jnp.pad(state_conv, ((0, 0), (0, 0), (CONV_HIST - (CONV_W - 1), 0), (0, 0)))

    xp = x_prompt.reshape(nb * t, d)
    xs = x_sample.reshape(ns * ts, d)
    p_state, s_state = [], []
    for l in range(depth):
        last = l == depth - 1
        mix_w = (conv_w[l], conv_b[l], conv_ln_g[l], conv_ln_b[l], wc16[l], wa16[l], wb16[l], wo16[l])

        xp = _ffn(xp, ln_ffn1[l], f1g[l], f1u[l], f1d[l], final_norm, final=False)
        (qa, ka32, va32, ka16, va16, qb, kb32, vb32, kb16, vb16, u, gate) = _inproj(
            xp, ln_mix[l], w_in16[l], b_gate[l], width=wa_width, seq_len=t)
        seq = lambda a: a.reshape(nb, t, a.shape[-1])
        oa = _band_attention(seq(qa), seq(ka16), seq(va16), bias_p[l], qoff=0)
        ob = _stick_breaking(seq(qb), seq(kb16), seq(vb16), qoff=0)
        u3 = seq(u)
        xp = _merge(u3, u3, oa, ob, seq(gate), seq(xp), *mix_w, tm=512, hist_from_u=True)
        xp = _ffn(xp.reshape(nb * t, d), ln_ffn2[l], f2g[l], f2u[l], f2d[l], final_norm, final=last)
        heads = lambda a, h: a.reshape(nb, h, HEAD_DIM, a.shape[-1]).transpose(0, 3, 1, 2)
        tail = lambda a, h: heads(a[:, :, t - a_len:], h)
        p_state.append((tail(ka32, h_a), tail(va32, h_a),
                        heads(kb32, h_b), heads(vb32, h_b), u3[:, t - (CONV_W - 1):]))

        xs = _ffn(xs, ln_ffn1[l], f1g[l], f1u[l], f1d[l], final_norm, final=False)
        (qa, ka32, va32, ka16, va16, qb, kb32, vb32, kb16, vb16, u, gate) = _inproj(
            xs, ln_mix[l], w_in16[l], b_gate[l], width=wa_width)
        seq = lambda a: a.reshape(ns, ts, a.shape[-1])
        with_new = lambda padded, at, new: padded[l].at[:, at:at + ts].set(seq(new))
        oa = _band_attention(seq(qa), with_new(seq_a_k, a_len, ka16),
                             with_new(seq_a_v, a_len, va16), bias_s[l], qoff=band_qoff_s)
        ob = _stick_breaking(seq(qb), with_new(seq_b_k, past, kb16),
                             with_new(seq_b_v, past, vb16), qoff=sb_qoff_s)
        u3 = seq(u)
        xs = _merge(u3, hist[l], oa, ob, seq(gate), seq(xs), *mix_w, tm=ts, hist_from_u=False)
        xs = _ffn(xs.reshape(ns * ts, d), ln_ffn2[l], f2g[l], f2u[l], f2d[l], final_norm, final=last)
        heads = lambda a, h: a.reshape(ns, ts, h, HEAD_DIM)
        up = jnp.concatenate([state_conv[l], u3], axis=1)
        s_state.append((heads(ka32, h_a), heads(va32, h_a), heads(kb32, h_b), heads(vb32, h_b),
                        up[:, ts:]))

    stack = lambda states, i: jnp.stack([s[i] for s in states], axis=0)
    return (xp.reshape(nb, t, d), xs.reshape(ns, ts, d),
            stack(p_state, 0), stack(p_state, 1), stack(p_state, 2), stack(p_state, 3), stack(p_state, 4),
            stack(s_state, 0), stack(s_state, 1), stack(s_state, 2), stack(s_state, 3), stack(s_state, 4))
```

```python
import functools

import numpy as np
import jax
import jax.numpy as jnp
from jax import lax
from jax.experimental import pallas as pl
from jax.experimental.pallas import tpu as pltpu

F32 = jnp.float32
BF16 = jnp.bfloat16

HEAD_DIM = 64
CHUNK = 64
A_PAST_CHUNKS = 8
REL_MIN = -(CHUNK - 1)
REL_MAX = 128
CONV_W = 31
RMS_EPS = 1e-6
LN_EPS = 1e-5
NEG_INF = -1e30
LOG2_E = 1.4426950408889634
Q_SCALE = HEAD_DIM ** -0.5 * LOG2_E

LANES = 128
SUBLANES = 8
MXU_EDGE = 256
BAND_BLOCK = 4 * CHUNK
BAND_NBLK = A_PAST_CHUNKS * CHUNK // BAND_BLOCK + 1
SB_BLOCK = 256
SB_HEADS = 4
SB_DEAD = -175.0
CONV_HIST = 32
VMEM_LIMIT = 56 * 1024 * 1024


def _params(n_axes, vmem=VMEM_LIMIT):
    return pltpu.CompilerParams(dimension_semantics=("arbitrary",) * n_axes, vmem_limit_bytes=vmem)


def _resident(shape):
    nd = len(shape)
    return pl.BlockSpec(shape, lambda *_: (0,) * nd, pipeline_mode=pl.Buffered(1))


def _of_layer(stacked, layer):
    tail = stacked.shape[1:]
    return pl.BlockSpec((None,) + tail, lambda *_: (layer,) + (0,) * len(tail),
                        pipeline_mode=pl.Buffered(1))


def _row_tile(m, want):
    tm = want
    while tm > 16 and m % tm:
        tm //= 2
    assert m % tm == 0
    return tm


def _rms(x, g):
    return x * lax.rsqrt(jnp.mean(x * x, axis=-1, keepdims=True) + RMS_EPS) * g


def _sigmoid(x):
    return 1.0 / (1.0 + jnp.exp(-x))


def _dot(a, b):
    return jnp.dot(a, b, preferred_element_type=F32)


def _dot_nt(a, b):
    return lax.dot_general(a, b, (((1,), (1,)), ((), ())), preferred_element_type=F32)


CAST_BLOCK_BYTES = 4 * 1024 * 1024


def _cast_body(x_ref, o_ref):
    o_ref[...] = x_ref[...].astype(BF16)


def _to_bf16(w):
    cols = w.shape[-1]
    w2 = w.reshape(-1, cols)
    rows = w2.shape[0]
    want = 1024
    while want * cols * 4 > CAST_BLOCK_BYTES:
        want //= 2
    tr = _row_tile(rows, want)
    spec = pl.BlockSpec((tr, cols), lambda i: (i, 0))
    out = pl.pallas_call(
        _cast_body,
        out_shape=jax.ShapeDtypeStruct((rows, cols), BF16),
        grid=(rows // tr,),
        in_specs=[spec],
        out_specs=spec,
        compiler_params=_params(1),
        name="cast_bf16",
    )(w2)
    return out.reshape(w.shape)


def _cache_rows_body(x_ref, o_ref):
    t = x_ref.shape[2]
    for lo in range(0, t, MXU_EDGE):
        o_ref[0, lo:lo + MXU_EDGE, :] = x_ref[0, :, lo:lo + MXU_EDGE].T.astype(BF16)
    o_ref[0, t:, :] = jnp.zeros((o_ref.shape[1] - t, o_ref.shape[2]), BF16)


def _cache_rows(cache, extra):
    depth, n, t, heads, hd = cache.shape
    w = heads * hd
    assert t % MXU_EDGE == 0 and extra % SUBLANES == 0
    x = cache.transpose(0, 1, 3, 4, 2).reshape(depth * n, w, t)
    out = pl.pallas_call(
        _cache_rows_body,
        out_shape=jax.ShapeDtypeStruct((depth * n, t + extra, w), BF16),
        grid=(depth * n,),
        in_specs=[pl.BlockSpec((1, w, t), lambda s: (s, 0, 0))],
        out_specs=pl.BlockSpec((1, t + extra, w), lambda s: (s, 0, 0)),
        compiler_params=_params(1),
        name="cache_rows",
    )(x)
    return out.reshape(depth, n, t + extra, w)


def _ffn_body(x_ref, g_ref, wg_ref, wu_ref, wd_ref, fg_ref, o_ref, *, cuts, final):
    x = x_ref[...]
    h = _rms(x, g_ref[...]).astype(BF16)
    acc = None
    for lo, hi in zip(cuts[:-1], cuts[1:]):
        a = _dot(h, wg_ref[:, lo:hi])
        b = _dot(h, wu_ref[:, lo:hi])
        act = (a * _sigmoid(a) * b).astype(BF16)
        d = _dot(act, wd_ref[lo:hi, :])
        acc = d if acc is None else acc + d
    y = x + 0.5 * acc
    if final:
        y = _rms(y, fg_ref[...])
    o_ref[...] = y


def _ffn(x, g, wg, wu, wd, fg, *, layer, final):
    m, d = x.shape
    d_ff = wg.shape[2]
    tm = _row_tile(m, 512)
    assert d_ff % MXU_EDGE == 0
    cuts = (0, (d_ff // MXU_EDGE + 1) // 2 * MXU_EDGE, d_ff)
    row = pl.BlockSpec((tm, d), lambda i: (i, 0))
    return pl.pallas_call(
        functools.partial(_ffn_body, cuts=cuts, final=final),
        out_shape=jax.ShapeDtypeStruct((m, d), F32),
        grid=(m // tm,),
        in_specs=[row] + [_of_layer(a, layer) for a in (g, wg, wu, wd)] + [_resident((1, d))],
        out_specs=row,
        compiler_params=_params(1),
        name="ffn_half",
    )(x, g, wg, wu, wd, fg.reshape(1, d))


def _inproj_body(x_ref, g_ref, bg_ref, *rest, width, d_model, time_minor):
    w_refs = rest[:-12]
    (qa_ref, ka32_ref, va32_ref, ka16_ref, va16_ref,
     qb_ref, kb32_ref, vb32_ref, kb16_ref, vb16_ref, u_ref, gate_ref) = rest[-12:]
    h = _rms(x_ref[...], g_ref[...]).astype(BF16)

    def mm(col, n):
        slab, at = divmod(col, d_model)
        assert at + n <= d_model
        return _dot(h, w_refs[slab][:, at:at + n])

    def put_state(ref, val):
        if time_minor:
            ref[0] = val.T
        else:
            ref[...] = val

    scale = Q_SCALE
    col = 0
    for q_ref, k32_ref, v32_ref, k16_ref, v16_ref in (
            (qa_ref, ka32_ref, va32_ref, ka16_ref, va16_ref),
            (qb_ref, kb32_ref, vb32_ref, kb16_ref, vb16_ref)):
        q_ref[...] = (mm(col, width) * scale).astype(BF16)
        k = mm(col + width, width)
        put_state(k32_ref, k)
        k16_ref[...] = k.astype(BF16)
        v = mm(col + 2 * width, width)
        put_state(v32_ref, v)
        v16_ref[...] = v.astype(BF16)
        col += 3 * width
    lin = mm(col, width)
    gt = mm(col + width, width)
    u_ref[...] = lin * _sigmoid(gt)
    col += 2 * width
    for j in range(3):
        graw = mm(col + j * d_model, d_model) + bg_ref[:, j * d_model:(j + 1) * d_model]
        gate_ref[:, j * d_model:(j + 1) * d_model] = _sigmoid(graw).astype(BF16)


def _inproj(x, g, w_in, b_gate, *, layer, width, seq_len=None):
    m, d = x.shape
    n_cols = w_in.shape[2]
    tm = _row_tile(m, 512)
    row = lambda n: pl.BlockSpec((tm, n), lambda i: (i, 0))
    sds = lambda n, dt: jax.ShapeDtypeStruct((m, n), dt)
    if seq_len is None:
        state, state_spec = sds(width, F32), row(width)
    else:
        per_seq = seq_len // tm
        state = jax.ShapeDtypeStruct((m // seq_len, width, seq_len), F32)
        state_spec = pl.BlockSpec((1, width, tm), lambda i: (i // per_seq, 0, i % per_seq))
    mixer = [sds(width, BF16), state, state, sds(width, BF16), sds(width, BF16)]
    mixer_specs = [row(width), state_spec, state_spec, row(width), row(width)]
    n_slabs = n_cols // d
    assert n_cols % d == 0 and d % width == 0
    slabs = [pl.BlockSpec((None, d, d), functools.partial(lambda i, s: (layer, 0, s), s=s),
                          pipeline_mode=pl.Buffered(1)) for s in range(n_slabs)]
    return pl.pallas_call(
        functools.partial(_inproj_body, width=width, d_model=d, time_minor=seq_len is not None),
        out_shape=mixer + mixer + [sds(width, F32), sds(3 * d, BF16)],
        grid=(m // tm,),
        in_specs=[row(d), _of_layer(g, layer), _of_layer(b_gate, layer)] + slabs,
        out_specs=mixer_specs + mixer_specs + [row(width), row(3 * d)],
        compiler_params=_params(1),
        name="in_projection",
    )(x, g, b_gate, *([w_in] * n_slabs))


def _band_body(q_ref, *rest, qoff, n_pairs):
    k_refs, v_refs = rest[:n_pairs], rest[n_pairs:2 * n_pairs]
    bias_ref, o_ref = rest[2 * n_pairs:]
    rows = q_ref.shape[1]
    blk = pl.program_id(1) + qoff
    low = lax.broadcasted_iota(jnp.int32, (rows, LANES), 1) < HEAD_DIM
    first = [blk + j - (BAND_NBLK - 1) for j in range(BAND_NBLK)]
    starts = [pl.multiple_of(jnp.maximum(f, 0) * BAND_BLOCK, BAND_BLOCK) for f in first]

    def attend(hp, window_complete):
        lanes = slice(hp * LANES, (hp + 1) * LANES)
        q = q_ref[0, :, lanes].astype(F32)
        q2 = jnp.concatenate([jnp.where(low, q, 0.0), jnp.where(low, 0.0, q)], axis=0).astype(BF16)
        s = []
        for j in range(BAND_NBLK):
            bias = bias_ref[hp, j]
            if rows < BAND_BLOCK:
                bias = jnp.concatenate([bias[0:rows], bias[BAND_BLOCK:BAND_BLOCK + rows]], axis=0)
            sj = _dot_nt(q2, k_refs[hp][0, pl.ds(starts[j], BAND_BLOCK), :]) + bias
            if j < BAND_NBLK - 1 and not window_complete:
                sj = jnp.where(first[j] >= 0, sj, NEG_INF)
            s.append(sj)
        mx = functools.reduce(jnp.maximum, [jnp.max(sj, axis=-1, keepdims=True) for sj in s])
        p = [jnp.exp2(sj - mx) for sj in s]
        den = functools.reduce(jnp.add, [jnp.sum(pj, axis=-1, keepdims=True) for pj in p])
        o = functools.reduce(jnp.add, [
            _dot(pj.astype(BF16), v_refs[hp][0, pl.ds(starts[j], BAND_BLOCK), :])
            for j, pj in enumerate(p)])
        o = o * (1.0 / den)
        o_ref[0, :, lanes] = jnp.where(low, o[:rows], o[rows:]).astype(BF16)

    @pl.when(first[0] >= 0)
    def _():
        for hp in range(n_pairs):
            attend(hp, True)

    @pl.when(first[0] < 0)
    def _():
        for hp in range(n_pairs):
            attend(hp, False)


def _band_attention(q, k, v, bias, *, layer, qoff):
    n, tq, w = q.shape
    tk = k.shape[1]
    n_pairs = w // LANES
    rows = min(tq, BAND_BLOCK)
    qspec = pl.BlockSpec((1, rows, w), lambda b, i: (b, i, 0))
    mode = dict(pipeline_mode=pl.Buffered(1)) if tq > BAND_BLOCK else {}
    kspecs = [pl.BlockSpec((1, tk, LANES), functools.partial(lambda b, i, hp: (b, 0, hp), hp=hp),
                           **mode) for hp in range(n_pairs)]
    return pl.pallas_call(
        functools.partial(_band_body, qoff=qoff, n_pairs=n_pairs),
        out_shape=jax.ShapeDtypeStruct((n, tq, w), BF16),
        grid=(n, tq // rows),
        in_specs=[qspec] + kspecs + kspecs + [_of_layer(bias, layer)],
        out_specs=qspec,
        compiler_params=_params(2),
        name="band_attention",
    )(q, *([k] * n_pairs), *([v] * n_pairs), bias)


def _band_bias(rel_bias, kv_len):
    nb = BAND_BLOCK
    r = np.arange(nb)[:, None]
    c = np.arange(nb)[None, :]
    ok = np.zeros((BAND_NBLK, nb, nb), bool)
    real = np.zeros((BAND_NBLK, nb, nb), bool)
    m = np.arange(2 * nb)
    diff = np.where(m < nb, -m, 2 * nb - m)
    line_idx = np.zeros((BAND_NBLK, 2 * nb), np.int32)
    for j in range(BAND_NBLK):
        line_idx[j] = np.clip(diff + nb * (BAND_NBLK - 1 - j), REL_MIN, REL_MAX) - REL_MIN
        dchunk = (A_PAST_CHUNKS + r // CHUNK) - (c // CHUNK + (nb // CHUNK) * j)
        ok[j] = (dchunk >= 0) & (dchunk <= A_PAST_CHUNKS)
        real[j] = np.broadcast_to((nb * j + c) < kv_len, (nb, nb))
    depth, heads = rel_bias.shape[:2]
    line = (rel_bias.astype(F32) * LOG2_E)[:, :, line_idx]
    line = line.reshape(depth, heads // 2, 2, BAND_NBLK, 2 * nb).transpose(0, 1, 3, 2, 4)
    table = jnp.tile(line, (1, 1, 1, 1, nb))[..., :nb * (2 * nb - 1)]
    table = table.reshape(depth, heads // 2, BAND_NBLK, 2, nb, 2 * nb - 1)[..., :nb]
    table = jnp.where(ok[None, None, :, None], table, NEG_INF)
    padded = jnp.where(real[None, None, :, None], table, NEG_INF)
    shape = (depth, heads // 2, BAND_NBLK, 2 * nb, nb)
    return table.reshape(shape), padded.reshape(shape)


def _sb_body(q_ref, k_ref, v_ref, tri_ref, o_ref, *, qoff):
    blk = pl.program_id(2) + qoff
    width = SB_HEADS * HEAD_DIM
    tq = q_ref.shape[1]
    rows = SB_HEADS * tq
    lane = lax.broadcasted_iota(jnp.int32, (tq, width), 1)
    q = q_ref[0].astype(F32)
    qs = jnp.concatenate(
        [jnp.where((lane >= h * HEAD_DIM) & (lane < (h + 1) * HEAD_DIM), q, 0.0)
         for h in range(SB_HEADS)], axis=0).astype(BF16)
    tri = tri_ref[...]
    row = lax.broadcasted_iota(jnp.int32, (rows, SB_BLOCK), 0) & (tq - 1)
    col = lax.broadcasted_iota(jnp.int32, (rows, SB_BLOCK), 1)
    causal = col < row
    low = lax.broadcasted_iota(jnp.int32, (tq, LANES), 1) < HEAD_DIM

    def sweep(kb, acc, later, diagonal, valid=None):
        start = pl.multiple_of(kb * SB_BLOCK, SB_BLOCK)
        k = k_ref[0, pl.ds(start, SB_BLOCK), :]
        v = v_ref[0, pl.ds(start, SB_BLOCK), :]
        z = _dot_nt(qs, k)
        drop = jnp.maximum(z, 0.0) + jnp.log2(1.0 + jnp.exp2(-jnp.abs(z)))
        if diagonal:
            drop = jnp.where(causal, drop, 0.0)
        suffix = _dot(drop.astype(BF16), tri)
        a = jnp.exp2(z + suffix + later)
        if diagonal:
            a = jnp.where(causal, a, 0.0)
        pv = _dot(a.astype(BF16), v)
        pieces = [pv[h * tq:(h + 1) * tq, (h // 2) * LANES:(h // 2 + 1) * LANES]
                  for h in range(SB_HEADS)]
        picked = jnp.concatenate([jnp.where(low, pieces[0], pieces[1]),
                                  jnp.where(low, pieces[2], pieces[3])], axis=1)
        total = suffix[:, 0:1]
        if valid is not None:
            picked = jnp.where(valid, picked, 0.0)
            total = jnp.where(valid, total, 0.0)
        return acc + picked, later + total

    acc = jnp.zeros((tq, width), F32)
    later = jnp.zeros((rows, 1), F32)
    acc, later = sweep(blk, acc, later, True)
    acc, later = sweep(jnp.maximum(blk - 1, 0), acc, later, False, valid=blk > 0)

    def body(carry):
        _, kb, acc, later = carry
        acc, later = sweep(kb, acc, later, False)
        alive = jnp.max(later) > SB_DEAD
        more = jnp.logical_and(kb > 0, alive).astype(jnp.int32)
        return more, kb - 1, acc, later

    start = jnp.logical_and(blk > 1, jnp.max(later) > SB_DEAD).astype(jnp.int32)
    _, _, acc, later = lax.while_loop(lambda c: c[0] > 0, body, (start, blk - 2, acc, later))
    o_ref[0] = acc.astype(BF16)


def _stick_breaking(q, k, v, *, qoff):
    n, tq, w = q.shape
    tk = k.shape[1]
    gw = SB_HEADS * HEAD_DIM
    rows = min(tq, SB_BLOCK)
    assert rows & (rows - 1) == 0
    tri = jnp.asarray(-np.tril(np.ones((SB_BLOCK, SB_BLOCK), np.float32)), BF16)
    qspec = pl.BlockSpec((1, rows, gw), lambda b, g, i: (b, i, g))
    kspec = pl.BlockSpec((1, tk, gw), lambda b, g, i: (b, 0, g))
    return pl.pallas_call(
        functools.partial(_sb_body, qoff=qoff),
        out_shape=jax.ShapeDtypeStruct((n, tq, w), BF16),
        grid=(n, w // gw, tq // rows),
        in_specs=[qspec, kspec, kspec, _resident((SB_BLOCK, SB_BLOCK))],
        out_specs=qspec,
        compiler_params=_params(3),
        name="stick_breaking",
    )(q, k, v, tri)


def _merge_body(u_ref, hist_ref, oa_ref, ob_ref, gate_ref, x_ref, cw_ref, cb_ref, lg_ref, lb_ref,
                wc_ref, wa_ref, wb_ref, wo_ref, o_ref, up_ref, rot_ref, *, zero_first_hist):
    tm = u_ref.shape[1]
    d = x_ref.shape[2]
    hist = hist_ref[0]
    if zero_first_hist:
        hist = jnp.where(pl.program_id(1) > 0, hist, 0.0)
    up_ref[0:CONV_HIST, :] = hist
    up_ref[CONV_HIST:CONV_HIST + tm, :] = u_ref[0]
    span = rot_ref.shape[1]
    for b in range(1, SUBLANES):
        rot_ref[b - 1] = up_ref[b:b + span, :]
    first = CONV_HIST - (CONV_W - 1)

    def rows_out(r0, n):
        y = None
        for w in range(CONV_W):
            a, b = divmod(first + w, SUBLANES)
            lo = SUBLANES * a + r0
            window = up_ref[lo:lo + n, :] if b == 0 else rot_ref[b - 1, lo:lo + n, :]
            term = window * cw_ref[w:w + 1, :]
            y = term if y is None else y + term
        y = y + cb_ref[...]
        mu = jnp.mean(y, axis=-1, keepdims=True)
        yc = y - mu
        var = jnp.mean(yc * yc, axis=-1, keepdims=True)
        yn = yc * lax.rsqrt(var + LN_EPS) * lg_ref[...] + lb_ref[...]
        oc = _dot((yn * _sigmoid(yn)).astype(BF16), wc_ref[...])
        pa = _dot(oa_ref[0, r0:r0 + n, :], wa_ref[...])
        pb = _dot(ob_ref[0, r0:r0 + n, :], wb_ref[...])
        merged = (gate_ref[0, r0:r0 + n, 0:d].astype(F32) * pa
                  + gate_ref[0, r0:r0 + n, d:2 * d].astype(F32) * pb
                  + gate_ref[0, r0:r0 + n, 2 * d:3 * d].astype(F32) * oc)
        o_ref[0, r0:r0 + n, :] = x_ref[0, r0:r0 + n, :] + _dot(merged.astype(BF16), wo_ref[...])

    halves = 2 if tm % (2 * MXU_EDGE) == 0 else 1
    for part in range(halves):
        rows_out(part * (tm // halves), tm // halves)


def _merge(u, hist, oa, ob, gate, x, cw, cb, lg, lb, wc, wa, wb, wo, *, layer, tm, hist_from_u):
    n, t, c = u.shape
    d = x.shape[2]
    tile = lambda ch: pl.BlockSpec((1, tm, ch), lambda b, i: (b, i, 0))
    if hist_from_u:
        per = tm // CONV_HIST
        hspec = pl.BlockSpec((1, CONV_HIST, c), lambda b, i: (b, jnp.maximum(i * per - 1, 0), 0))
    else:
        hspec = pl.BlockSpec((1, CONV_HIST, c), lambda b, i: (b, 0, 0))
    params = (cw, cb, lg, lb, wc, wa, wb, wo)
    return pl.pallas_call(
        functools.partial(_merge_body, zero_first_hist=hist_from_u),
        out_shape=jax.ShapeDtypeStruct((n, t, d), F32),
        grid=(n, t // tm),
        in_specs=[tile(c), hspec, tile(oa.shape[2]), tile(ob.shape[2]), tile(3 * d), tile(d)]
        + [_of_layer(a, layer) for a in params],
        out_specs=tile(d),
        scratch_shapes=[pltpu.VMEM((CONV_HIST + tm, c), F32),
                        pltpu.VMEM((SUBLANES - 1, CONV_HIST + tm - SUBLANES, c), F32)],
        compiler_params=_params(2),
        name="conv_merge",
    )(u, hist, oa, ob, gate, x, *params)


def kernel(x_prompt, x_sample, cache_a_k, cache_a_v, cache_b_k, cache_b_v, state_conv, w_in, b_gate,
           rel_bias, w_a_out, w_b_out, conv_w, conv_b, conv_ln_g, conv_ln_b, w_c_out, w_o, ln_ffn1,
           ffn1_w_gate, ffn1_w_up, ffn1_w_down, ln_mix, ln_ffn2, ffn2_w_gate, ffn2_w_up, ffn2_w_down,
           final_norm):
    depth = w_in.shape[0]
    nb, t, d = x_prompt.shape
    ns, ts, _ = x_sample.shape
    a_len = cache_a_k.shape[2]
    past = cache_b_k.shape[2]
    h_a, h_b = cache_a_k.shape[3], cache_b_k.shape[3]
    wa_width, wb_width = h_a * HEAD_DIM, h_b * HEAD_DIM
    c_conv = state_conv.shape[3]
    assert wa_width == wb_width == c_conv
    assert t % 512 == 0 and a_len % BAND_BLOCK == 0 and past % SB_BLOCK == 0
    assert a_len == A_PAST_CHUNKS * CHUNK and past % CHUNK == 0 and ts <= CHUNK and ts % 16 == 0
    assert t >= a_len

    w_in16, wa16, wb16, wc16, wo16 = map(_to_bf16, (w_in, w_a_out, w_b_out, w_c_out, w_o))
    f1g, f1u, f1d = map(_to_bf16, (ffn1_w_gate, ffn1_w_up, ffn1_w_down))
    f2g, f2u, f2d = map(_to_bf16, (ffn2_w_gate, ffn2_w_up, ffn2_w_down))

    bias_p, bias_s = _band_bias(rel_bias, kv_len=a_len + ts)
    band_qoff_s = a_len // BAND_BLOCK
    sb_qoff_s = past // SB_BLOCK

    seq_a_k, seq_a_v = _cache_rows(cache_a_k, BAND_BLOCK), _cache_rows(cache_a_v, BAND_BLOCK)
    seq_b_k, seq_b_v = _cache_rows(cache_b_k, SB_BLOCK), _cache_rows(cache_b_v, SB_BLOCK)
    hist = jnp.pad(state_conv, ((0, 0), (0, 0), (CONV_HIST - (CONV_W - 1), 0), (0, 0)))

    xp = x_prompt.reshape(nb * t, d)
    xs = x_sample.reshape(ns * ts, d)
    p_state, s_state = [], []
    vec = lambda a: a.reshape(depth, 1, -1)
    ffn1 = (vec(ln_ffn1), f1g, f1u, f1d, final_norm)
    ffn2 = (vec(ln_ffn2), f2g, f2u, f2d, final_norm)
    proj = (vec(ln_mix), w_in16, vec(b_gate))
    mix_w = (conv_w, vec(conv_b), vec(conv_ln_g), vec(conv_ln_b), wc16, wa16, wb16, wo16)
    for l in range(depth):
        last = l == depth - 1

        xp = _ffn(xp, *ffn1, layer=l, final=False)
        (qa, ka32, va32, ka16, va16, qb, kb32, vb32, kb16, vb16, u, gate) = _inproj(
            xp, *proj, layer=l, width=wa_width, seq_len=t)
        seq = lambda a: a.reshape(nb, t, a.shape[-1])
        oa = _band_attention(seq(qa), seq(ka16), seq(va16), bias_p, layer=l, qoff=0)
        ob = _stick_breaking(seq(qb), seq(kb16), seq(vb16), qoff=0)
        u3 = seq(u)
        xp = _merge(u3, u3, oa, ob, seq(gate), seq(xp), *mix_w, layer=l, tm=512, hist_from_u=True)
        xp = _ffn(xp.reshape(nb * t, d), *ffn2, layer=l, final=last)
        heads = lambda a, h: a.reshape(nb, h, HEAD_DIM, a.shape[-1]).transpose(0, 3, 1, 2)
        tail = lambda a, h: heads(a[:, :, t - a_len:], h)
        p_state.append((tail(ka32, h_a), tail(va32, h_a),
                        heads(kb32, h_b), heads(vb32, h_b), u3[:, t - (CONV_W - 1):]))

        xs = _ffn(xs, *ffn1, layer=l, final=False)
        (qa, ka32, va32, ka16, va16, qb, kb32, vb32, kb16, vb16, u, gate) = _inproj(
            xs, *proj, layer=l, width=wa_width)
        seq = lambda a: a.reshape(ns, ts, a.shape[-1])
        with_new = lambda padded, at, new: padded[l].at[:, at:at + ts].set(seq(new))
        oa = _band_attention(seq(qa), with_new(seq_a_k, a_len, ka16),
                             with_new(seq_a_v, a_len, va16), bias_s, layer=l, qoff=band_qoff_s)
        ob = _stick_breaking(seq(qb), with_new(seq_b_k, past, kb16),
                             with_new(seq_b_v, past, vb16), qoff=sb_qoff_s)
        u3 = seq(u)
        xs = _merge(u3, hist[l], oa, ob, seq(gate), seq(xs), *mix_w, layer=l, tm=ts, hist_from_u=False)
        xs = _ffn(xs.reshape(ns * ts, d), *ffn2, layer=l, final=last)
        heads = lambda a, h: a.reshape(ns, ts, h, HEAD_DIM)
        up = jnp.concatenate([state_conv[l], u3], axis=1)
        s_state.append((heads(ka32, h_a), heads(va32, h_a), heads(kb32, h_b), heads(vb32, h_b),
                        up[:, ts:]))

    stack = lambda states, i: jnp.stack([s[i] for s in states], axis=0)
    return (xp.reshape(nb, t, d), xs.reshape(ns, ts, d),
            stack(p_state, 0), stack(p_state, 1), stack(p_state, 2), stack(p_state, 3), stack(p_state, 4),
            stack(s_state, 0), stack(s_state, 1), stack(s_state, 2), stack(s_state, 3), stack(s_state, 4))
```
